```python
import jax, jax.numpy as jnp
from jax import lax
import numpy as np

D_MODEL = 2048
BATCH = 2
SEQ = 16384
DEPTH = 4

N_A_LAYERS = DEPTH // 2
N_B_LAYERS = DEPTH - N_A_LAYERS
D_RNN = (4 * D_MODEL // 3) // 128 * 128
LRU_BLOCKS = 16
LRU_BLOCK_W = D_RNN // LRU_BLOCKS
CONV_W = 4
LRU_C = 8.0
HEAD_DIM = 128
ROT_DIM = HEAD_DIM // 4
ROPE_THETA = 500000.0
DIL_GROUPS = ((128, 1), (512, 4), (2048, 16))
N_GROUPS = len(DIL_GROUPS)
HEADS_PER_GROUP = D_MODEL // 256
ATTN_W = N_GROUPS * HEADS_PER_GROUP * HEAD_DIM
ATTN_OUT_W = HEADS_PER_GROUP * HEAD_DIM
N_EXPERTS = 32
TOP_K = 4
D_FF_EXPERT = 3 * D_MODEL // 8
SWIGLU_LIMIT = 7.0
SWIGLU_ALPHA = 1.702
MOE_BLOCK = 128
DN_ALPHA = (2 * DEPTH) ** 0.25
DN_BETA = (8 * DEPTH) ** -0.25
LN_EPS = 1e-5

kernel_name = 'rglru_dilated_attn_moe_hybrid'


def layer_norm(x, g, b):
    xf = x.astype(jnp.float32)
    mu = xf.mean(-1, keepdims=True)
    var = jnp.square(xf - mu).mean(-1, keepdims=True)
    return ((xf - mu) * lax.rsqrt(var + LN_EPS) * g.astype(jnp.float32) + b.astype(jnp.float32)).astype(x.dtype)


def rotary(t, positions):
    half = ROT_DIM // 2
    inv = jnp.power(ROPE_THETA, -jnp.arange(half, dtype=jnp.float32) * (2.0 / ROT_DIM))
    ang = positions.astype(jnp.float32)[..., None] * inv
    cos = jnp.cos(ang)[:, :, None, :]
    sin = jnp.sin(ang)[:, :, None, :]
    tr = t[..., :ROT_DIM].astype(jnp.float32)
    t1, t2 = tr[..., :half], tr[..., half:]
    rot = jnp.concatenate([t1 * cos - t2 * sin, t2 * cos + t1 * sin], axis=-1).astype(t.dtype)
    return jnp.concatenate([rot, t[..., ROT_DIM:]], axis=-1)


def rglru_mixer(x, w_in, conv_w, conv_b, w_rg, b_rg, w_ig, b_ig, lam, w_out):
    B, S, _ = x.shape
    proj = x @ w_in
    gate = jax.nn.gelu(proj[..., :D_RNN])
    u = proj[..., D_RNN:]
    u_pad = jnp.pad(u, ((0, 0), (CONV_W - 1, 0), (0, 0)))
    u = conv_b + sum(u_pad[:, k:k + S] * conv_w[k] for k in range(CONV_W))
    ub = u.reshape(B, S, LRU_BLOCKS, LRU_BLOCK_W)
    r = jax.nn.sigmoid((jnp.einsum('bsnc,ncd->bsnd', ub, w_rg).reshape(B, S, D_RNN) + b_rg).astype(jnp.float32))
    i = jax.nn.sigmoid((jnp.einsum('bsnc,ncd->bsnd', ub, w_ig).reshape(B, S, D_RNN) + b_ig).astype(jnp.float32))
    log_a = -LRU_C * r * jax.nn.softplus(-lam.astype(jnp.float32))
    a = jnp.exp(log_a)
    b_in = jnp.sqrt(-jnp.expm1(2.0 * log_a)) * i * u.astype(jnp.float32)

    def step(h, ab):
        h = ab[0] * h + ab[1]
        return h, h

    _, h = lax.scan(step, jnp.zeros((B, D_RNN), jnp.float32), (jnp.swapaxes(a, 0, 1), jnp.swapaxes(b_in, 0, 1)))
    h = jnp.swapaxes(h, 0, 1)
    return (h.astype(x.dtype) * gate) @ w_out


def to_strided(t, d):
    B, S = t.shape[:2]
    return jnp.swapaxes(t.reshape(B, S // d, d, *t.shape[2:]), 1, 2)


def from_strided(t):
    B, N, L = t.shape[:3]
    return jnp.swapaxes(t, 1, 2).reshape(B, N * L, *t.shape[3:])


def banded_attention(q, k, v, blk):
    B, N, L, H, Dh = q.shape
    nb = -(-L // blk)
    Lp = nb * blk
    qb = jnp.pad(q, ((0, 0), (0, 0), (0, Lp - L), (0, 0), (0, 0))).reshape(B, N, nb, blk, H, Dh)

    def two_blocks(t):
        tp = jnp.pad(t, ((0, 0), (0, 0), (blk, Lp - L), (0, 0), (0, 0)))
        prev = tp[:, :, :Lp].reshape(B, N, nb, blk, H, Dh)
        cur = tp[:, :, blk:].reshape(B, N, nb, blk, H, Dh)
        return jnp.concatenate([prev, cur], axis=3)

    kb, vb = two_blocks(k), two_blocks(v)
    s = jnp.einsum('bnkqhd,bnkjhd->bnkhqj', qb, kb, preferred_element_type=jnp.float32) * (HEAD_DIM ** -0.5)
    qi = jnp.arange(blk)[:, None] + blk
    kj = jnp.arange(2 * blk)[None, :]
    dist = qi - kj
    kabs = jnp.arange(nb)[:, None, None] * blk + kj[None] - blk
    mask = ((dist >= 0) & (dist <= blk))[None] & (kabs >= 0)
    s = jnp.where(mask[:, None], s, -jnp.inf)
    m = s.max(-1, keepdims=True)
    p = jnp.exp(s - m)
    den = p.sum(-1, keepdims=True)
    o = jnp.einsum('bnkhqj,bnkjhd->bnkqhd', p, vb.astype(jnp.float32))
    o = o / jnp.swapaxes(den[..., 0], 3, 4)[..., None]
    lse = jnp.swapaxes((m + jnp.log(den))[..., 0], 3, 4)
    o = o.reshape(B, N, Lp, H, Dh)[:, :, :L]
    lse = lse.reshape(B, N, Lp, H)[:, :, :L]
    return o, lse


def shared_kv(x, positions, kv_w):
    B, S, _ = x.shape
    kv = (x @ kv_w).reshape(B, S, 2, N_GROUPS * HEADS_PER_GROUP, HEAD_DIM)
    k = rotary(kv[:, :, 0], positions).reshape(B, S, N_GROUPS, HEADS_PER_GROUP, HEAD_DIM)
    v = kv[:, :, 1].reshape(B, S, N_GROUPS, HEADS_PER_GROUP, HEAD_DIM)
    return k, v


def dilated_attention(x, positions, k_sh, v_sh, w_q, w_o):
    B, S, _ = x.shape
    q = (x @ w_q).reshape(B, S, N_GROUPS, HEADS_PER_GROUP, HEAD_DIM)
    outs, lses = [], []
    for g, (window, dil) in enumerate(DIL_GROUPS):
        qg = rotary(q[:, :, g], positions)
        o, lse = banded_attention(to_strided(qg, dil), to_strided(k_sh[:, :, g], dil),
                                  to_strided(v_sh[:, :, g], dil), window // dil)
        outs.append(from_strided(o))
        lses.append(from_strided(lse))
    wts = jax.nn.softmax(jnp.stack(lses), axis=0)
    o = jnp.einsum('gbsh,gbshd->bshd', wts, jnp.stack(outs))
    return o.reshape(B, S, ATTN_OUT_W).astype(x.dtype) @ w_o


def moe(x, w_router, b_router, w_gu, b_gu, w_down, b_down):
    B, S, D = x.shape
    T = B * S
    xf = x.reshape(T, D)
    logits = (xf @ w_router + b_router).astype(jnp.float32)
    top_v, top_e = lax.top_k(logits, TOP_K)
    gates = jax.nn.softmax(top_v, axis=-1)
    e_flat = top_e.reshape(-1)
    order = jnp.argsort(e_flat)
    e_sorted = e_flat[order]
    tok_sorted = order // TOP_K
    g_sorted = gates.reshape(-1)[order]
    counts = jnp.bincount(e_flat, length=N_EXPERTS)
    padded = (counts + MOE_BLOCK - 1) // MOE_BLOCK * MOE_BLOCK
    pad_end = jnp.cumsum(padded)
    pad_start = pad_end - padded
    cnt_start = jnp.cumsum(counts) - counts
    dest = pad_start[e_sorted] + jnp.arange(T * TOP_K) - cnt_start[e_sorted]
    n_blocks = -(-(T * TOP_K) // MOE_BLOCK) + N_EXPERTS
    R = n_blocks * MOE_BLOCK
    row_src = jnp.zeros((R,), jnp.int32).at[dest].set(tok_sorted.astype(jnp.int32))
    row_tok = jnp.full((R,), T, jnp.int32).at[dest].set(tok_sorted.astype(jnp.int32))
    row_gate = jnp.zeros((R,), jnp.float32).at[dest].set(g_sorted)
    block_e = jnp.minimum(jnp.searchsorted(pad_end, jnp.arange(n_blocks) * MOE_BLOCK, side='right'), N_EXPERTS - 1)

    def expert_block(args):
        src, gate, e = args
        xb = xf[src]
        gu = xb @ w_gu[e] + b_gu[e]
        glu = jnp.minimum(gu[:, :D_FF_EXPERT], SWIGLU_LIMIT)
        lin = jnp.clip(gu[:, D_FF_EXPERT:], -SWIGLU_LIMIT, SWIGLU_LIMIT)
        h = glu * jax.nn.sigmoid(SWIGLU_ALPHA * glu) * (lin + 1.0)
        return (h @ w_down[e] + b_down[e]) * gate[:, None].astype(x.dtype)

    y_rows = lax.map(expert_block, (row_src.reshape(n_blocks, MOE_BLOCK),
                                    row_gate.reshape(n_blocks, MOE_BLOCK), block_e))
    y = jax.ops.segment_sum(y_rows.reshape(R, D), row_tok, num_segments=T)
    return y.reshape(B, S, D)


def setup_inputs(seed: int = 0) -> dict:
    key = jax.random.key(seed)
    ks = jax.random.split(key, 24)
    f32 = jnp.float32
    nrm = lambda k, shape, s: jax.random.normal(k, shape, f32) * s
    x = jax.random.normal(ks[0], (BATCH, SEQ, D_MODEL), f32)
    positions = jnp.tile(jnp.arange(SEQ, dtype=jnp.int32)[None], (BATCH, 1))
    a0 = jax.random.uniform(ks[9], (N_A_LAYERS, D_RNN), f32, 0.9, 0.999)
    kv_k = nrm(ks[11], (D_MODEL, ATTN_W), D_MODEL ** -0.5)
    kv_v = nrm(ks[12], (D_MODEL, ATTN_W), D_MODEL ** -0.5 * DN_BETA)
    return {
        'x': x,
        'positions': positions,
        'a_w_in': nrm(ks[1], (N_A_LAYERS, D_MODEL, 2 * D_RNN), D_MODEL ** -0.5),
        'a_conv_w': nrm(ks[2], (N_A_LAYERS, CONV_W, D_RNN), CONV_W ** -0.5),
        'a_conv_b': nrm(ks[3], (N_A_LAYERS, D_RNN), 0.01),
        'a_w_rg': nrm(ks[4], (N_A_LAYERS, LRU_BLOCKS, LRU_BLOCK_W, LRU_BLOCK_W), LRU_BLOCK_W ** -0.5),
        'a_b_rg': nrm(ks[5], (N_A_LAYERS, D_RNN), 0.01),
        'a_w_ig': nrm(ks[6], (N_A_LAYERS, LRU_BLOCKS, LRU_BLOCK_W, LRU_BLOCK_W), LRU_BLOCK_W ** -0.5),
        'a_b_ig': nrm(ks[7], (N_A_LAYERS, D_RNN), 0.01),
        'a_lambda': jnp.log(a0) - jnp.log1p(-a0),
        'a_w_out': nrm(ks[8], (N_A_LAYERS, D_RNN, D_MODEL), D_RNN ** -0.5 * DN_BETA),
        'kv_w': jnp.concatenate([kv_k, kv_v], axis=1),
        'b_w_q': nrm(ks[13], (N_B_LAYERS, D_MODEL, ATTN_W), D_MODEL ** -0.5),
        'b_w_o': nrm(ks[14], (N_B_LAYERS, ATTN_OUT_W, D_MODEL), ATTN_OUT_W ** -0.5 * DN_BETA),
        'mix_ln_g': 1.0 + nrm(ks[15], (DEPTH, D_MODEL), 0.02),
        'mix_ln_b': nrm(ks[16], (DEPTH, D_MODEL), 0.02),
        'moe_w_router': nrm(ks[17], (DEPTH, D_MODEL, N_EXPERTS), D_MODEL ** -0.5),
        'moe_b_router': nrm(ks[18], (DEPTH, N_EXPERTS), 0.01),
        'moe_w_gu': nrm(ks[19], (DEPTH, N_EXPERTS, D_MODEL, 2 * D_FF_EXPERT), D_MODEL ** -0.5),
        'moe_b_gu': nrm(ks[20], (DEPTH, N_EXPERTS, 2 * D_FF_EXPERT), 0.01),
        'moe_w_down': nrm(ks[21], (DEPTH, N_EXPERTS, D_FF_EXPERT, D_MODEL), D_FF_EXPERT ** -0.5 * DN_BETA),
        'moe_b_down': nrm(ks[22], (DEPTH, N_EXPERTS, D_MODEL), 0.01),
        'ffn_ln_g': 1.0 + nrm(ks[23], (DEPTH, D_MODEL), 0.02),
        'ffn_ln_b': nrm(ks[10], (DEPTH, D_MODEL), 0.02),
    }


def reference(x, positions, a_w_in, a_conv_w, a_conv_b, a_w_rg, a_b_rg, a_w_ig, a_b_ig, a_lambda,
              a_w_out, kv_w, b_w_q, b_w_o, mix_ln_g, mix_ln_b, moe_w_router, moe_b_router,
              moe_w_gu, moe_b_gu, moe_w_down, moe_b_down, ffn_ln_g, ffn_ln_b):
    k_sh, v_sh = None, None
    for layer in range(DEPTH):
        if layer < N_A_LAYERS:
            mix = rglru_mixer(x, a_w_in[layer], a_conv_w[layer], a_conv_b[layer], a_w_rg[layer],
                              a_b_rg[layer], a_w_ig[layer], a_b_ig[layer], a_lambda[layer], a_w_out[layer])
        else:
            j = layer - N_A_LAYERS
            mix = dilated_attention(x, positions, k_sh, v_sh, b_w_q[j], b_w_o[j])
        x = layer_norm(DN_ALPHA * x + mix, mix_ln_g[layer], mix_ln_b[layer])
        ffn = moe(x, moe_w_router[layer], moe_b_router[layer], moe_w_gu[layer], moe_b_gu[layer],
                  moe_w_down[layer], moe_b_down[layer])
        x = layer_norm(DN_ALPHA * x + ffn, ffn_ln_g[layer], ffn_ln_b[layer])
        if layer == N_A_LAYERS - 1:
            k_sh, v_sh = shared_kv(x, positions, kv_w)
    return x
```

```python
import functools

import jax
import jax.numpy as jnp
from jax import lax
from jax.experimental import pallas as pl
from jax.experimental.pallas import tpu as pltpu

F32 = jnp.float32
BF16 = jnp.bfloat16

CONV_W = 4
LRU_C = 8.0
HEAD_DIM = 128
ROT_DIM = HEAD_DIM // 4
ROPE_THETA = 500000.0
DIL_GROUPS = ((128, 1), (512, 4), (2048, 16))
ATTN_BLK = 128
ATTN_SUPER = 2048
TOP_K = 4
SWIGLU_LIMIT = 7.0
SWIGLU_ALPHA = 1.702
LN_EPS = 1e-5

LANES = 128
SUBLANES = 8
VMEM_LIMIT = 56 * 1024 * 1024


def _params(sem):
    return pltpu.CompilerParams(dimension_semantics=sem, vmem_limit_bytes=VMEM_LIMIT)


def _pick(n, candidates):
    for c in candidates:
        if n % c == 0:
            return c
    raise ValueError(f"no tile in {candidates} divides {n}")


def _layer_norm(y, g, b):
    mu = jnp.mean(y, axis=-1, keepdims=True)
    yc = y - mu
    var = jnp.mean(yc * yc, axis=-1, keepdims=True)
    return yc * lax.rsqrt(var + LN_EPS) * g + b


def _rope_chunk(c, cos_t, sin_t):
    lane = lax.broadcasted_iota(jnp.int32, c.shape, 1)
    partner = jnp.where(lane < ROT_DIM // 2,
                        pltpu.roll(c, LANES - ROT_DIM // 2, 1),
                        pltpu.roll(c, ROT_DIM // 2, 1))
    return c * cos_t + partner * sin_t


def _proj_kernel(x_ref, w_ref, *rest, epilogue, n_special):
    if epilogue == "rope":
        cos_ref, sin_ref, o_ref = rest
    else:
        (o_ref,) = rest
    acc = jnp.dot(x_ref[...].astype(BF16), w_ref[...], preferred_element_type=F32)
    j = pl.program_id(1)

    @pl.when(j >= n_special)
    def _():
        o_ref[...] = acc

    @pl.when(j < n_special)
    def _():
        if epilogue == "gelu":
            o_ref[...] = 0.5 * acc * (1.0 + jnp.tanh(
                0.7978845608028654 * (acc + 0.044715 * (acc * acc * acc))))
        else:
            cos_t = cos_ref[...]
            sin_t = sin_ref[...]
            for h in range(acc.shape[1] // LANES):
                sl = slice(h * LANES, (h + 1) * LANES)
                o_ref[:, sl] = _rope_chunk(acc[:, sl], cos_t, sin_t)


def _proj(x, w, *, tn, epilogue, n_special_cols, cos_t=None, sin_t=None, tm=512):
    m, k = x.shape
    n = w.shape[1]
    assert m % tm == 0 and n % tn == 0 and n_special_cols % tn == 0
    in_specs = [pl.BlockSpec((tm, k), lambda i, j: (i, 0)),
                pl.BlockSpec((k, tn), lambda i, j: (0, j))]
    args = [x, w]
    if epilogue == "rope":
        in_specs += [pl.BlockSpec((tm, LANES), lambda i, j: (i, 0))] * 2
        args += [cos_t, sin_t]
    return pl.pallas_call(
        functools.partial(_proj_kernel, epilogue=epilogue, n_special=n_special_cols // tn),
        grid=(m // tm, n // tn),
        in_specs=in_specs,
        out_specs=pl.BlockSpec((tm, tn), lambda i, j: (i, j)),
        out_shape=jax.ShapeDtypeStruct((m, n), F32),
        compiler_params=_params(("parallel", "arbitrary")),
        name=f"proj_{epilogue}",
    )(*args)


def _band_plan(c, bw, ct):
    plan = []
    for j in range(c // ct):
        c0 = j * ct
        lo = (c0 // bw) * bw
        hi = ((c0 + ct - 1) // bw + 1) * bw
        ks = (lo // LANES) * LANES
        kb = -(-(hi - ks) // LANES) * LANES
        ks = min(ks, c - kb)
        plan.append((ks, kb))
    return plan


def _band_weights(w_blocks, plan, ct):
    nb, bw, _ = w_blocks.shape
    c = nb * bw
    dense = jnp.zeros((nb, bw, nb, bw), w_blocks.dtype)
    dense = dense.at[jnp.arange(nb), :, jnp.arange(nb), :].set(w_blocks)
    dense = dense.reshape(c, c)
    kb_max = max(kb for _, kb in plan)
    slabs = []
    for j, (ks, kb) in enumerate(plan):
        slab = dense[ks:ks + kb, j * ct:(j + 1) * ct]
        slabs.append(jnp.pad(slab, ((0, kb_max - kb), (0, 0))))
    return jnp.stack(slabs).astype(BF16)


def _scan8(a, b):
    row = lax.broadcasted_iota(jnp.int32, a.shape, 0)
    for d in (1, 2, 4):
        a_prev = pltpu.roll(a, d, 0)
        b_prev = pltpu.roll(b, d, 0)
        keep = row >= d
        b = jnp.where(keep, a * b_prev + b, b)
        a = jnp.where(keep, a * a_prev, a)
    return a, b


def _rglru_kernel(gate_ref, u_ref, cw_ref, cb_ref, wrg_ref, brg_ref, wig_ref, big_ref,
                  lam_ref, o_ref, ubuf, uc_s, a_s, b_s, hcarry, *, plan, ct):
    ts = u_ref.shape[0]
    i = pl.program_id(1)

    @pl.when(i == 0)
    def _():
        ubuf[0:SUBLANES, :] = jnp.zeros((SUBLANES, ubuf.shape[1]), F32)
        hcarry[...] = jnp.zeros(hcarry.shape, F32)

    ubuf[SUBLANES:SUBLANES + ts, :] = u_ref[...]
    base = SUBLANES - (CONV_W - 1)
    conv = ubuf[base:base + ts, :] * cw_ref[0:1, :]
    for k in range(1, CONV_W):
        conv = conv + ubuf[base + k:base + k + ts, :] * cw_ref[k:k + 1, :]
    uc_s[...] = cb_ref[...] + conv
    ubuf[0:SUBLANES, :] = ubuf[ts:ts + SUBLANES, :]

    lam = lam_ref[...]
    softplus_neg_lam = jnp.maximum(-lam, 0.0) + jnp.log1p(jnp.exp(-jnp.abs(lam)))

    for j, (ks, kb) in enumerate(plan):
        cs = slice(j * ct, (j + 1) * ct)
        band = uc_s[:, ks:ks + kb].astype(BF16)
        r = jax.nn.sigmoid(jnp.dot(band, wrg_ref[j, 0:kb, :], preferred_element_type=F32)
                           + brg_ref[:, cs])
        ig = jax.nn.sigmoid(jnp.dot(band, wig_ref[j, 0:kb, :], preferred_element_type=F32)
                            + big_ref[:, cs])
        log_a = -LRU_C * r * softplus_neg_lam[:, cs]
        a = jnp.exp(log_a)
        a_s[...] = a
        b_s[...] = jnp.sqrt(-jnp.tanh(log_a) * (a * a + 1.0)) * ig * uc_s[:, cs]

        def group(g, carry, cs=cs):
            halves = []
            for s in range(2):
                rows = pl.ds(pl.multiple_of(g * 2 * SUBLANES + s * SUBLANES, SUBLANES), SUBLANES)
                a8, b8 = _scan8(a_s[rows, :], b_s[rows, :])
                h8 = b8 + a8 * carry
                carry = jnp.broadcast_to(h8[SUBLANES - 1:SUBLANES, :], h8.shape)
                halves.append(h8 * gate_ref[rows, cs])
            rows16 = pl.ds(pl.multiple_of(g * 2 * SUBLANES, 2 * SUBLANES), 2 * SUBLANES)
            o_ref[rows16, cs] = jnp.concatenate(halves, axis=0).astype(o_ref.dtype)
            return carry

        hcarry[:, cs] = lax.fori_loop(0, ts // (2 * SUBLANES), group, hcarry[:, cs])


def _rglru(proj, conv_w, conv_b, w_rg, b_rg, w_ig, b_ig, lam, *, batch, ts=256):
    t, c2 = proj.shape
    c = c2 // 2
    nb, bw, _ = w_rg.shape
    ct = _pick(c, (384, 256, 128))
    plan = _band_plan(c, bw, ct)
    wrg = _band_weights(w_rg, plan, ct)
    wig = _band_weights(w_ig, plan, ct)
    seq = t // batch
    assert seq % ts == 0
    nt = seq // ts
    row = lambda v: v.reshape(1, c)
    vec = pl.BlockSpec((1, c), lambda b, i: (0, 0))
    slab = pl.BlockSpec(wrg.shape, lambda b, i: (0, 0, 0))
    return pl.pallas_call(
        functools.partial(_rglru_kernel, plan=plan, ct=ct),
        grid=(batch, nt),
        in_specs=[pl.BlockSpec((ts, c), lambda b, i: (b * nt + i, 0)),
                  pl.BlockSpec((ts, c), lambda b, i: (b * nt + i, 1)),
                  pl.BlockSpec((CONV_W, c), lambda b, i: (0, 0)),
                  vec, slab, vec, slab, vec, vec],
        out_specs=pl.BlockSpec((ts, c), lambda b, i: (b * nt + i, 0)),
        out_shape=jax.ShapeDtypeStruct((t, c), BF16),
        scratch_shapes=[pltpu.VMEM((ts + SUBLANES, c), F32),
                        pltpu.VMEM((ts, c), F32),
                        pltpu.VMEM((ts, ct), F32),
                        pltpu.VMEM((ts, ct), F32),
                        pltpu.VMEM((SUBLANES, c), F32)],
        compiler_params=_params(("arbitrary", "arbitrary")),
        name="rglru",
    )(proj, proj, conv_w, row(conv_b), wrg, row(b_rg), wig, row(b_ig), row(lam))


def _mix_ln_router_kernel(a_ref, w_ref, res_ref, g_ref, b_ref, wr_ref, br_ref,
                          xn_ref, e_ref, gate_ref, *, alpha):
    mix = jnp.dot(a_ref[...], w_ref[...], preferred_element_type=F32)
    xn = _layer_norm(alpha * res_ref[...] + mix, g_ref[...], b_ref[...])
    xn_ref[...] = xn
    logits = jnp.dot(xn, wr_ref[...], precision=lax.Precision.HIGHEST,
                     preferred_element_type=F32) + br_ref[...]
    n_exp = logits.shape[1]
    col = lax.broadcasted_iota(jnp.int32, logits.shape, 1).astype(F32)
    slot = lax.broadcasted_iota(jnp.int32, e_ref.shape, 1)
    e_out = jnp.zeros(e_ref.shape, F32)
    v_out = jnp.zeros(e_ref.shape, F32)
    for k in range(TOP_K):
        top = jnp.max(logits, axis=-1, keepdims=True)
        idx = jnp.min(jnp.where(logits == top, col, float(n_exp)), axis=-1, keepdims=True)
        logits = jnp.where(col == idx, -jnp.inf, logits)
        e_out = jnp.where(slot == k, idx, e_out)
        v_out = jnp.where(slot == k, top, v_out)
    p = jnp.exp(v_out - jnp.max(v_out, axis=-1, keepdims=True))
    e_ref[...] = e_out.astype(jnp.int32)
    gate_ref[...] = p / jnp.sum(p, axis=-1, keepdims=True)


def _mix_ln_router(a, w, res, g, b, w_router, b_router, *, alpha, tm=256):
    m, k = a.shape
    d = w.shape[1]
    n_exp = w_router.shape[1]
    assert m % tm == 0
    full = lambda shape: pl.BlockSpec(shape, lambda i: (0,) * len(shape))
    return pl.pallas_call(
        functools.partial(_mix_ln_router_kernel, alpha=alpha),
        grid=(m // tm,),
        in_specs=[pl.BlockSpec((tm, k), lambda i: (i, 0)), full((k, d)),
                  pl.BlockSpec((tm, d), lambda i: (i, 0)), full((1, d)), full((1, d)),
                  full((d, n_exp)), full((1, n_exp))],
        out_specs=[pl.BlockSpec((tm, d), lambda i: (i, 0)),
                   pl.BlockSpec((tm, TOP_K), lambda i: (i, 0)),
                   pl.BlockSpec((tm, TOP_K), lambda i: (i, 0))],
        out_shape=[jax.ShapeDtypeStruct((m, d), F32),
                   jax.ShapeDtypeStruct((m, TOP_K), jnp.int32),
                   jax.ShapeDtypeStruct((m, TOP_K), F32)],
        compiler_params=_params(("parallel",)),
        name="mix_ln_router",
    )(a, w, res, g.reshape(1, d), b.reshape(1, d), w_router, b_router.reshape(1, n_exp))


def _dispatch_kernel(dest_ref, x_ref, zeros_ref, xs_ref, sems):
    del zeros_ref
    tt = x_ref.shape[0]
    i = pl.program_id(0)

    def row_copy(r, d, k):
        return pltpu.make_async_copy(x_ref.at[pl.ds(r, 1), :], xs_ref.at[pl.ds(d, 1), :],
                                     sems.at[k])

    def issue(r, carry):
        for k in range(TOP_K):
            row_copy(r, dest_ref[(i * tt + r) * TOP_K + k], k).start()
        return carry

    lax.fori_loop(0, tt, issue, 0)

    for k in range(TOP_K):
        pltpu.make_async_copy(x_ref, xs_ref.at[pl.ds(0, tt), :], sems.at[k]).wait()


def _dispatch(x, dest_flat, n_rows, *, tt=256):
    t, d = x.shape
    assert t % tt == 0
    grid_spec = pltpu.PrefetchScalarGridSpec(
        num_scalar_prefetch=1,
        grid=(t // tt,),
        in_specs=[pl.BlockSpec((tt, d), lambda i, dest: (i, 0)),
                  pl.BlockSpec(memory_space=pl.ANY)],
        out_specs=pl.BlockSpec(memory_space=pl.ANY),
        scratch_shapes=[pltpu.SemaphoreType.DMA((TOP_K,))],
    )
    return pl.pallas_call(
        _dispatch_kernel,
        grid_spec=grid_spec,
        out_shape=jax.ShapeDtypeStruct((n_rows, d), F32),
        input_output_aliases={2: 0},
        compiler_params=_params(("arbitrary",)),
        name="moe_dispatch",
    )(dest_flat, x, jnp.zeros((n_rows, d), F32))


def _expert_kernel(block_e_ref, n_valid_ref, xs_ref, wgu_ref, bgu_ref, wd_ref, bd_ref, y_ref):
    del block_e_ref
    f = wd_ref.shape[0]

    live = pl.program_id(0) < n_valid_ref[0]

    @pl.when(jnp.logical_not(live))
    def _():
        y_ref[...] = jnp.zeros(y_ref.shape, y_ref.dtype)

    @pl.when(live)
    def _():
        gu = jnp.dot(xs_ref[...].astype(BF16), wgu_ref[...],
                     preferred_element_type=F32) + bgu_ref[...]
        glu = jnp.minimum(gu[:, :f], SWIGLU_LIMIT)
        lin = jnp.clip(gu[:, f:], -SWIGLU_LIMIT, SWIGLU_LIMIT)
        h = glu * jax.nn.sigmoid(SWIGLU_ALPHA * glu) * (lin + 1.0)
        y_ref[...] = jnp.dot(h.astype(BF16), wd_ref[...],
                             preferred_element_type=F32) + bd_ref[...]


def _experts(xs, block_e, n_valid, w_gu, b_gu, w_down, b_down, *, tm):
    r, d = xs.shape
    n_exp, _, f2 = w_gu.shape
    f = f2 // 2
    nb = r // tm
    live = lambda i, be, nv: jnp.minimum(i, nv[0] - 1)
    grid_spec = pltpu.PrefetchScalarGridSpec(
        num_scalar_prefetch=2,
        grid=(nb,),
        in_specs=[pl.BlockSpec((tm, d), lambda i, be, nv: (live(i, be, nv), 0)),
                  pl.BlockSpec((None, d, f2), lambda i, be, nv: (be[i], 0, 0)),
                  pl.BlockSpec((None, 1, f2), lambda i, be, nv: (be[i], 0, 0)),
                  pl.BlockSpec((None, f, d), lambda i, be, nv: (be[i], 0, 0)),
                  pl.BlockSpec((None, 1, d), lambda i, be, nv: (be[i], 0, 0))],
        out_specs=pl.BlockSpec((tm, d), lambda i, be, nv: (i, 0)),
    )
    return pl.pallas_call(
        _expert_kernel,
        grid_spec=grid_spec,
        out_shape=jax.ShapeDtypeStruct((r, d), F32),
        compiler_params=_params(("arbitrary",)),
        name="moe_experts",
    )(block_e, n_valid, xs, w_gu, b_gu.reshape(n_exp, 1, f2), w_down,
      b_down.reshape(n_exp, 1, d))


def _combine_ln_kernel(dest_ref, y_ref, gate_ref, res_ref, g_ref, b_ref, o_ref, buf, sems,
                       *, alpha):
    tt = res_ref.shape[0]
    i = pl.program_id(0)
    n = pl.num_programs(0)
    slot = i % 2

    def row_copy(step, r, k, s):
        src = dest_ref[(step * tt + r) * TOP_K + k]
        return pltpu.make_async_copy(y_ref.at[pl.ds(src, 1), :], buf.at[s, k, pl.ds(r, 1), :],
                                     sems.at[s, k])

    def issue(step, s):
        def body(r, carry):
            for k in range(TOP_K):
                row_copy(step, r, k, s).start()
            return carry
        lax.fori_loop(0, tt, body, 0)

    @pl.when(i == 0)
    def _():
        issue(0, 0)

    @pl.when(i + 1 < n)
    def _():
        issue(i + 1, 1 - slot)

    for k in range(TOP_K):
        pltpu.make_async_copy(y_ref.at[pl.ds(0, tt), :], buf.at[slot, k], sems.at[slot, k]).wait()

    gate = gate_ref[...]
    ffn = buf[slot, 0] * gate[:, 0:1]
    for k in range(1, TOP_K):
        ffn = ffn + buf[slot, k] * gate[:, k:k + 1]
    o_ref[...] = _layer_norm(alpha * res_ref[...] + ffn, g_ref[...], b_ref[...])


def _combine_ln(y_rows, dest_flat, gates, res, g, b, *, alpha, tt=128):
    t, d = res.shape
    assert t % tt == 0
    grid_spec = pltpu.PrefetchScalarGridSpec(
        num_scalar_prefetch=1,
        grid=(t // tt,),
        in_specs=[pl.BlockSpec(memory_space=pl.ANY),
                  pl.BlockSpec((tt, TOP_K), lambda i, dest: (i, 0)),
                  pl.BlockSpec((tt, d), lambda i, dest: (i, 0)),
                  pl.BlockSpec((1, d), lambda i, dest: (0, 0)),
                  pl.BlockSpec((1, d), lambda i, dest: (0, 0))],
        out_specs=pl.BlockSpec((tt, d), lambda i, dest: (i, 0)),
        scratch_shapes=[pltpu.VMEM((2, TOP_K, tt, d), F32),
                        pltpu.SemaphoreType.DMA((2, TOP_K))],
    )
    return pl.pallas_call(
        functools.partial(_combine_ln_kernel, alpha=alpha),
        grid_spec=grid_spec,
        out_shape=jax.ShapeDtypeStruct((t, d), F32),
        compiler_params=_params(("arbitrary",)),
        name="moe_combine_ln",
    )(dest_flat, y_rows, gates, res, g.reshape(1, d), b.reshape(1, d))


def _route(top_e, n_exp, tm, n_blocks):
    sel = jnp.sum(top_e[:, :, None] == jnp.arange(n_exp, dtype=jnp.int32), axis=1,
                  dtype=jnp.int32)
    before = jnp.cumsum(sel, axis=0) - sel
    counts = before[-1] + sel[-1]
    padded = (counts + tm - 1) // tm * tm
    pad_end = jnp.cumsum(padded)
    pad_start = pad_end - padded
    rank = jnp.take_along_axis(before, top_e, axis=1)
    dest = (pad_start[top_e] + rank).astype(jnp.int32)
    block_e = jnp.minimum(
        jnp.searchsorted(pad_end, jnp.arange(n_blocks, dtype=jnp.int32) * tm, side="right"),
        n_exp - 1).astype(jnp.int32)
    n_valid = (pad_end[-1] // tm).astype(jnp.int32).reshape(1)
    return dest.reshape(-1), block_e, n_valid


def _moe_ln(x, top_e, gates, w_gu, b_gu, w_down, b_down, g, b, *, alpha, tm=256):
    t, _ = x.shape
    n_exp = w_gu.shape[0]
    n_blocks = t * TOP_K // tm + n_exp
    dest, block_e, n_valid = _route(top_e, n_exp, tm, n_blocks)
    xs = _dispatch(x, dest, n_blocks * tm)
    y_rows = _experts(xs, block_e, n_valid, w_gu, b_gu, w_down, b_down, tm=tm)
    return _combine_ln(y_rows, dest, gates, x, g, b, alpha=alpha)


def _attn_kernel(*refs, scale):
    ng = len(DIL_GROUPS)
    q_refs = refs[0:ng]
    kc_refs = refs[ng:2 * ng]
    kp_refs = refs[2 * ng:3 * ng]
    vc_refs = refs[3 * ng:4 * ng]
    vp_refs = refs[4 * ng:5 * ng]
    o_ref = refs[5 * ng]
    og_s, lse_s = refs[5 * ng + 1:]
    not_first = jnp.full((ATTN_BLK, ATTN_BLK), pl.program_id(1), jnp.int32) > 0
    blk = ATTN_BLK
    qi = lax.broadcasted_iota(jnp.int32, (blk, blk), 0)
    kj = lax.broadcasted_iota(jnp.int32, (blk, blk), 1)
    cur_ok = kj <= qi
    prev_ok = kj >= qi
    nt = (((1,), (1,)), ((), ()))

    for g, (_, dil) in enumerate(DIL_GROUPS):
        n_sub = ATTN_SUPER // (dil * blk)
        for r in range(dil):
            for kb in range(n_sub):
                def rows(ref, start):
                    if dil == 1:
                        return ref[start:start + blk, :]
                    return ref[pl.ds(start, blk, stride=dil), :]
                start = r + dil * blk * kb
                q = rows(q_refs[g], start).astype(BF16)
                k_cur = rows(kc_refs[g], start).astype(BF16)
                v_cur = rows(vc_refs[g], start).astype(BF16)
                if kb > 0:
                    pstart = start - dil * blk
                    k_prev = rows(kc_refs[g], pstart).astype(BF16)
                    v_prev = rows(vc_refs[g], pstart).astype(BF16)
                    p_ok = prev_ok
                else:
                    pstart = r + dil * blk * (n_sub - 1)
                    k_prev = rows(kp_refs[g], pstart).astype(BF16)
                    v_prev = rows(vp_refs[g], pstart).astype(BF16)
                    p_ok = prev_ok & not_first
                s_cur = lax.dot_general(q, k_cur, nt, preferred_element_type=F32) * scale
                s_prev = lax.dot_general(q, k_prev, nt, preferred_element_type=F32) * scale
                s_cur = jnp.where(cur_ok, s_cur, -jnp.inf)
                s_prev = jnp.where(p_ok, s_prev, -jnp.inf)
                m = jnp.maximum(jnp.max(s_cur, axis=-1, keepdims=True),
                                jnp.max(s_prev, axis=-1, keepdims=True))
                p_cur = jnp.exp(s_cur - m)
                p_prev = jnp.exp(s_prev - m)
                den = (jnp.sum(p_cur, axis=-1, keepdims=True)
                       + jnp.sum(p_prev, axis=-1, keepdims=True))
                o = (jnp.dot(p_cur.astype(BF16), v_cur, preferred_element_type=F32)
                     + jnp.dot(p_prev.astype(BF16), v_prev, preferred_element_type=F32)) / den
                lse = jnp.broadcast_to(m + jnp.log(den), (blk, LANES))
                if dil == 1:
                    og_s[g, start:start + blk, :] = o
                    lse_s[g, start:start + blk, :] = lse
                else:
                    og_s[g, pl.ds(start, blk, stride=dil), :] = o
                    lse_s[g, pl.ds(start, blk, stride=dil), :] = lse

    top = lse_s[0]
    for g in range(1, ng):
        top = jnp.maximum(top, lse_s[g])
    num = jnp.zeros(top.shape, F32)
    den = jnp.zeros(top.shape, F32)
    for g in range(ng):
        w = jnp.exp(lse_s[g] - top)
        num = num + w * og_s[g]
        den = den + w
    o_ref[...] = (num / den).astype(o_ref.dtype)


def _attention(q, kv, *, batch, hpg):
    t = q.shape[0]
    seq = t // batch
    assert seq % ATTN_SUPER == 0
    ns = seq // ATTN_SUPER
    ng = len(DIL_GROUPS)
    nh = ng * hpg

    def cur(off):
        return lambda b, m, h, off=off: (b * ns + m, off + h)

    def prev(off):
        return lambda b, m, h, off=off: (b * ns + jnp.maximum(m - 1, 0), off + h)

    blk = lambda index_map: pl.BlockSpec((ATTN_SUPER, HEAD_DIM), index_map)
    in_specs = ([blk(cur(g * hpg)) for g in range(ng)]
                + [blk(cur(g * hpg)) for g in range(ng)]
                + [blk(prev(g * hpg)) for g in range(ng)]
                + [blk(cur(nh + g * hpg)) for g in range(ng)]
                + [blk(prev(nh + g * hpg)) for g in range(ng)])
    return pl.pallas_call(
        functools.partial(_attn_kernel, scale=HEAD_DIM ** -0.5),
        grid=(batch, ns, hpg),
        in_specs=in_specs,
        out_specs=pl.BlockSpec((ATTN_SUPER, HEAD_DIM), lambda b, m, h: (b * ns + m, h)),
        out_shape=jax.ShapeDtypeStruct((t, hpg * HEAD_DIM), BF16),
        scratch_shapes=[pltpu.VMEM((ng, ATTN_SUPER, HEAD_DIM), F32),
                        pltpu.VMEM((ng, ATTN_SUPER, LANES), F32)],
        compiler_params=_params(("parallel", "arbitrary", "arbitrary")),
        name="dilated_attn",
    )(*([q] * ng + [kv] * (4 * ng)))


def _rope_tables(positions):
    half = ROT_DIM // 2
    inv = jnp.power(ROPE_THETA, -jnp.arange(half, dtype=F32) * (2.0 / ROT_DIM))
    ang = positions.reshape(-1).astype(F32)[:, None] * inv
    cos, sin = jnp.cos(ang), jnp.sin(ang)
    rest = HEAD_DIM - ROT_DIM
    cos_t = jnp.concatenate([cos, cos, jnp.ones((ang.shape[0], rest), F32)], axis=1)
    sin_t = jnp.concatenate([-sin, sin, jnp.zeros((ang.shape[0], rest), F32)], axis=1)
    return cos_t, sin_t


def kernel(x, positions, a_w_in, a_conv_w, a_conv_b, a_w_rg, a_b_rg, a_w_ig, a_b_ig, a_lambda,
           a_w_out, kv_w, b_w_q, b_w_o, mix_ln_g, mix_ln_b, moe_w_router, moe_b_router,
           moe_w_gu, moe_b_gu, moe_w_down, moe_b_down, ffn_ln_g, ffn_ln_b):
    batch, seq, d = x.shape
    depth = mix_ln_g.shape[0]
    n_a = a_w_in.shape[0]
    c = a_w_in.shape[2] // 2
    attn_w = b_w_q.shape[2]
    hpg = b_w_o.shape[1] // HEAD_DIM
    alpha = (2 * depth) ** 0.25
    t = batch * seq

    x = x.reshape(t, d)
    cos_t, sin_t = _rope_tables(positions)
    head_tn = _pick(attn_w, (768, 512, 384, 256, 128))
    kv = None
    for layer in range(depth):
        if layer < n_a:
            proj = _proj(x, a_w_in[layer].astype(BF16), tn=_pick(c, (896, 640, 512, 384, 256, 128)),
                         epilogue="gelu", n_special_cols=c)
            mixed = _rglru(proj, a_conv_w[layer], a_conv_b[layer], a_w_rg[layer], a_b_rg[layer],
                           a_w_ig[layer], a_b_ig[layer], a_lambda[layer], batch=batch)
            w_out = a_w_out[layer]
        else:
            j = layer - n_a
            q = _proj(x, b_w_q[j].astype(BF16), tn=head_tn, epilogue="rope",
                      n_special_cols=attn_w, cos_t=cos_t, sin_t=sin_t)
            mixed = _attention(q, kv, batch=batch, hpg=hpg)
            w_out = b_w_o[j]
        x, top_e, gates = _mix_ln_router(mixed, w_out.astype(BF16), x, mix_ln_g[layer],
                                         mix_ln_b[layer], moe_w_router[layer],
                                         moe_b_router[layer], alpha=alpha)
        x = _moe_ln(x, top_e, gates, moe_w_gu[layer].astype(BF16), moe_b_gu[layer],
                    moe_w_down[layer].astype(BF16), moe_b_down[layer], ffn_ln_g[layer],
                    ffn_ln_b[layer], alpha=alpha)
        if layer == n_a - 1:
            kv = _proj(x, kv_w.astype(BF16), tn=head_tn, epilogue="rope",
                       n_special_cols=attn_w, cos_t=cos_t, sin_t=sin_t)
    return x.reshape(batch, seq, d)
```

```python
import functools

import jax
import jax.numpy as jnp
from jax import lax
from jax.experimental import pallas as pl
from jax.experimental.pallas import tpu as pltpu

F32 = jnp.float32
BF16 = jnp.bfloat16

CONV_W = 4
LRU_C = 8.0
HEAD_DIM = 128
ROT_DIM = HEAD_DIM // 4
ROPE_THETA = 500000.0
DIL_GROUPS = ((128, 1), (512, 4), (2048, 16))
ATTN_BLK = 128
ATTN_SUPER = 2048
TOP_K = 4
SWIGLU_LIMIT = 7.0
SWIGLU_ALPHA = 1.702
LN_EPS = 1e-5

LANES = 128
SUBLANES = 8
VMEM_LIMIT = 56 * 1024 * 1024


def _params(sem):
    return pltpu.CompilerParams(dimension_semantics=sem, vmem_limit_bytes=VMEM_LIMIT)


def _pick(n, candidates):
    for c in candidates:
        if n % c == 0:
            return c
    raise ValueError(f"no tile in {candidates} divides {n}")


def _layer_norm(y, g, b):
    mu = jnp.mean(y, axis=-1, keepdims=True)
    yc = y - mu
    var = jnp.mean(yc * yc, axis=-1, keepdims=True)
    return yc * lax.rsqrt(var + LN_EPS) * g + b


def _rope_chunk(c, cos_t, sin_t):
    lane = lax.broadcasted_iota(jnp.int32, c.shape, 1)
    partner = jnp.where(lane < ROT_DIM // 2,
                        pltpu.roll(c, LANES - ROT_DIM // 2, 1),
                        pltpu.roll(c, ROT_DIM // 2, 1))
    return c * cos_t + partner * sin_t


def _proj_kernel(x_ref, w_ref, *rest, epilogue, n_special):
    if epilogue == "rope":
        cos_ref, sin_ref, o_ref, xb_ref = rest
    else:
        o_ref, xb_ref = rest
    j = pl.program_id(1)

    @pl.when(j == 0)
    def _():
        xb_ref[...] = x_ref[...].astype(BF16)

    acc = jnp.dot(xb_ref[...], w_ref[...], preferred_element_type=F32)

    @pl.when(j >= n_special)
    def _():
        o_ref[...] = acc

    @pl.when(j < n_special)
    def _():
        if epilogue == "gelu":
            o_ref[...] = 0.5 * acc * (1.0 + jnp.tanh(
                0.7978845608028654 * (acc + 0.044715 * (acc * acc * acc))))
        else:
            cos_t = cos_ref[...]
            sin_t = sin_ref[...]
            for h in range(acc.shape[1] // LANES):
                sl = slice(h * LANES, (h + 1) * LANES)
                o_ref[:, sl] = _rope_chunk(acc[:, sl], cos_t, sin_t)


def _proj(x, w, *, tn, epilogue, n_special_cols, cos_t=None, sin_t=None, tm=512):
    m, k = x.shape
    n = w.shape[1]
    assert m % tm == 0 and n % tn == 0 and n_special_cols % tn == 0
    in_specs = [pl.BlockSpec((tm, k), lambda i, j: (i, 0)),
                pl.BlockSpec((k, tn), lambda i, j: (0, j))]
    args = [x, w]
    if epilogue == "rope":
        in_specs += [pl.BlockSpec((tm, LANES), lambda i, j: (i, 0))] * 2
        args += [cos_t, sin_t]
    return pl.pallas_call(
        functools.partial(_proj_kernel, epilogue=epilogue, n_special=n_special_cols // tn),
        grid=(m // tm, n // tn),
        in_specs=in_specs,
        out_specs=pl.BlockSpec((tm, tn), lambda i, j: (i, j)),
        out_shape=jax.ShapeDtypeStruct((m, n), F32),
        scratch_shapes=[pltpu.VMEM((tm, k), BF16)],
        compiler_params=_params(("parallel", "arbitrary")),
        name=f"proj_{epilogue}",
    )(*args)


def _band_plan(c, bw, ct):
    plan = []
    for j in range(c // ct):
        c0 = j * ct
        lo = (c0 // bw) * bw
        hi = ((c0 + ct - 1) // bw + 1) * bw
        ks = (lo // LANES) * LANES
        kb = -(-(hi - ks) // LANES) * LANES
        ks = min(ks, c - kb)
        plan.append((ks, kb))
    return plan


def _band_weights(w_blocks, plan, ct):
    nb, bw, _ = w_blocks.shape
    c = nb * bw
    dense = jnp.zeros((nb, bw, nb, bw), w_blocks.dtype)
    dense = dense.at[jnp.arange(nb), :, jnp.arange(nb), :].set(w_blocks)
    dense = dense.reshape(c, c)
    kb_max = max(kb for _, kb in plan)
    slabs = []
    for j, (ks, kb) in enumerate(plan):
        slab = dense[ks:ks + kb, j * ct:(j + 1) * ct]
        slabs.append(jnp.pad(slab, ((0, kb_max - kb), (0, 0))))
    return jnp.stack(slabs).astype(BF16)


def _scan8(a, b):
    row = lax.broadcasted_iota(jnp.int32, a.shape, 0)
    for d in (1, 2, 4):
        a_prev = pltpu.roll(a, d, 0)
        b_prev = pltpu.roll(b, d, 0)
        keep = row >= d
        b = jnp.where(keep, a * b_prev + b, b)
        a = jnp.where(keep, a * a_prev, a)
    return a, b


def _rglru_kernel(gate_ref, u_ref, cw_ref, cb_ref, wrg_ref, brg_ref, wig_ref, big_ref,
                  lam_ref, o_ref, ubuf, uc_s, a_s, b_s, hcarry, *, plan, ct):
    ts = u_ref.shape[0]
    i = pl.program_id(1)

    @pl.when(i == 0)
    def _():
        ubuf[0:SUBLANES, :] = jnp.zeros((SUBLANES, ubuf.shape[1]), F32)
        hcarry[...] = jnp.zeros(hcarry.shape, F32)

    ubuf[SUBLANES:SUBLANES + ts, :] = u_ref[...]
    base = SUBLANES - (CONV_W - 1)
    conv = ubuf[base:base + ts, :] * cw_ref[0:1, :]
    for k in range(1, CONV_W):
        conv = conv + ubuf[base + k:base + k + ts, :] * cw_ref[k:k + 1, :]
    uc_s[...] = cb_ref[...] + conv
    ubuf[0:SUBLANES, :] = ubuf[ts:ts + SUBLANES, :]

    lam = lam_ref[...]
    softplus_neg_lam = jnp.maximum(-lam, 0.0) + jnp.log1p(jnp.exp(-jnp.abs(lam)))

    for j, (ks, kb) in enumerate(plan):
        cs = slice(j * ct, (j + 1) * ct)
        band = uc_s[:, ks:ks + kb].astype(BF16)
        r = jax.nn.sigmoid(jnp.dot(band, wrg_ref[j, 0:kb, :], preferred_element_type=F32)
                           + brg_ref[:, cs])
        ig = jax.nn.sigmoid(jnp.dot(band, wig_ref[j, 0:kb, :], preferred_element_type=F32)
                            + big_ref[:, cs])
        log_a = -LRU_C * r * softplus_neg_lam[:, cs]
        a = jnp.exp(log_a)
        a_s[...] = a
        b_s[...] = jnp.sqrt(-jnp.tanh(log_a) * (a * a + 1.0)) * ig * uc_s[:, cs]

        def group(g, carry, cs=cs):
            halves = []
            for s in range(2):
                rows = pl.ds(pl.multiple_of(g * 2 * SUBLANES + s * SUBLANES, SUBLANES), SUBLANES)
                a8, b8 = _scan8(a_s[rows, :], b_s[rows, :])
                h8 = b8 + a8 * carry
                carry = jnp.broadcast_to(h8[SUBLANES - 1:SUBLANES, :], h8.shape)
                halves.append(h8 * gate_ref[rows, cs])
            rows16 = pl.ds(pl.multiple_of(g * 2 * SUBLANES, 2 * SUBLANES), 2 * SUBLANES)
            o_ref[rows16, cs] = jnp.concatenate(halves, axis=0).astype(o_ref.dtype)
            return carry

        hcarry[:, cs] = lax.fori_loop(0, ts // (2 * SUBLANES), group, hcarry[:, cs])


def _rglru(proj, conv_w, conv_b, w_rg, b_rg, w_ig, b_ig, lam, *, batch, ts=256):
    t, c2 = proj.shape
    c = c2 // 2
    nb, bw, _ = w_rg.shape
    ct = _pick(c, (384, 256, 128))
    plan = _band_plan(c, bw, ct)
    wrg = _band_weights(w_rg, plan, ct)
    wig = _band_weights(w_ig, plan, ct)
    seq = t // batch
    assert seq % ts == 0
    nt = seq // ts
    row = lambda v: v.reshape(1, c)
    vec = pl.BlockSpec((1, c), lambda b, i: (0, 0))
    slab = pl.BlockSpec(wrg.shape, lambda b, i: (0, 0, 0))
    return pl.pallas_call(
        functools.partial(_rglru_kernel, plan=plan, ct=ct),
        grid=(batch, nt),
        in_specs=[pl.BlockSpec((ts, c), lambda b, i: (b * nt + i, 0)),
                  pl.BlockSpec((ts, c), lambda b, i: (b * nt + i, 1)),
                  pl.BlockSpec((CONV_W, c), lambda b, i: (0, 0)),
                  vec, slab, vec, slab, vec, vec],
        out_specs=pl.BlockSpec((ts, c), lambda b, i: (b * nt + i, 0)),
        out_shape=jax.ShapeDtypeStruct((t, c), BF16),
        scratch_shapes=[pltpu.VMEM((ts + SUBLANES, c), F32),
                        pltpu.VMEM((ts, c), F32),
                        pltpu.VMEM((ts, ct), F32),
                        pltpu.VMEM((ts, ct), F32),
                        pltpu.VMEM((SUBLANES, c), F32)],
        compiler_params=_params(("arbitrary", "arbitrary")),
        name="rglru",
    )(proj, proj, conv_w, row(conv_b), wrg, row(b_rg), wig, row(b_ig), row(lam))


def _mix_ln_router_kernel(a_ref, w_ref, res_ref, g_ref, b_ref, wr2_ref, br_ref,
                          xn_ref, e_ref, gate_ref, rank_ref, count_ref, run_s, *, alpha):
    tm = a_ref.shape[0]
    n_exp = br_ref.shape[1]

    @pl.when(pl.program_id(0) == 0)
    def _():
        run_s[...] = jnp.zeros(run_s.shape, F32)

    mix = jnp.dot(a_ref[...], w_ref[...], preferred_element_type=F32)
    xn = _layer_norm(alpha * res_ref[...] + mix, g_ref[...], b_ref[...])
    xn_ref[...] = xn

    x_hi = xn.astype(BF16)
    x_lo = (xn - x_hi.astype(F32)).astype(BF16)
    prod = jnp.dot(jnp.concatenate([x_hi, x_lo], axis=0), wr2_ref[...],
                   preferred_element_type=F32)
    logits = ((prod[:tm, n_exp:] + prod[tm:, :n_exp] + prod[tm:, n_exp:]) + prod[:tm, :n_exp]
              + br_ref[...])

    col = lax.broadcasted_iota(jnp.int32, logits.shape, 1).astype(F32)
    slot = lax.broadcasted_iota(jnp.int32, e_ref.shape, 1)
    e_out = jnp.zeros(e_ref.shape, F32)
    v_out = jnp.zeros(e_ref.shape, F32)
    picks = []
    for k in range(TOP_K):
        top = jnp.max(logits, axis=-1, keepdims=True)
        idx = jnp.min(jnp.where(logits == top, col, float(n_exp)), axis=-1, keepdims=True)
        picks.append(col == idx)
        logits = jnp.where(picks[k], -jnp.inf, logits)
        e_out = jnp.where(slot == k, idx, e_out)
        v_out = jnp.where(slot == k, top, v_out)
    p = jnp.exp(v_out - jnp.max(v_out, axis=-1, keepdims=True))
    e_ref[...] = e_out.astype(jnp.int32)
    gate_ref[...] = p / jnp.sum(p, axis=-1, keepdims=True)

    sel = jnp.zeros(logits.shape, F32)
    for k in range(TOP_K):
        sel = jnp.where(picks[k], 1.0, sel)
    earlier = (lax.broadcasted_iota(jnp.int32, (tm, tm), 1)
               < lax.broadcasted_iota(jnp.int32, (tm, tm), 0))
    before = jnp.dot(jnp.where(earlier, 1.0, 0.0).astype(BF16), sel.astype(BF16),
                     preferred_element_type=F32) + run_s[...]
    rank = jnp.zeros(e_ref.shape, F32)
    for k in range(TOP_K):
        rank_k = jnp.sum(jnp.where(picks[k], before, 0.0), axis=-1, keepdims=True)
        rank = jnp.where(slot == k, rank_k, rank)
    rank_ref[...] = rank.astype(jnp.int32)
    run_s[...] = run_s[...] + jnp.sum(sel, axis=0, keepdims=True)
    count_ref[...] = run_s[...]


def _mix_ln_router(a, w, res, g, b, w_router, b_router, *, alpha, tm=256):
    m, k = a.shape
    d = w.shape[1]
    n_exp = w_router.shape[1]
    assert m % tm == 0
    w_hi = w_router.astype(BF16)
    w_lo = (w_router - w_hi.astype(F32)).astype(BF16)
    full = lambda shape: pl.BlockSpec(shape, lambda i: (0,) * len(shape))
    return pl.pallas_call(
        functools.partial(_mix_ln_router_kernel, alpha=alpha),
        grid=(m // tm,),
        in_specs=[pl.BlockSpec((tm, k), lambda i: (i, 0)), full((k, d)),
                  pl.BlockSpec((tm, d), lambda i: (i, 0)), full((1, d)), full((1, d)),
                  full((d, 2 * n_exp)), full((1, n_exp))],
        out_specs=[pl.BlockSpec((tm, d), lambda i: (i, 0)),
                   pl.BlockSpec((tm, TOP_K), lambda i: (i, 0)),
                   pl.BlockSpec((tm, TOP_K), lambda i: (i, 0)),
                   pl.BlockSpec((tm, TOP_K), lambda i: (i, 0)),
                   full((1, n_exp))],
        out_shape=[jax.ShapeDtypeStruct((m, d), F32),
                   jax.ShapeDtypeStruct((m, TOP_K), jnp.int32),
                   jax.ShapeDtypeStruct((m, TOP_K), F32),
                   jax.ShapeDtypeStruct((m, TOP_K), jnp.int32),
                   jax.ShapeDtypeStruct((1, n_exp), F32)],
        scratch_shapes=[pltpu.VMEM((1, n_exp), F32)],
        compiler_params=_params(("arbitrary",)),
        name="mix_ln_router",
    )(a, w, res, g.reshape(1, d), b.reshape(1, d), jnp.concatenate([w_hi, w_lo], axis=1),
      b_router.reshape(1, n_exp))


def _store_slabs(dst_ref, val):
    n, d = val.shape
    sr = d // LANES
    for s in range(sr):
        dst_ref[pl.ds(s, n, stride=sr), :] = val[:, s * LANES:(s + 1) * LANES]


def _load_slabs(src_ref, n):
    sr = src_ref.shape[0] // n
    return jnp.concatenate([src_ref[pl.ds(s, n, stride=sr), :] for s in range(sr)], axis=1)


def _slab(ref, row, sr):
    return ref.at[pl.ds(pl.multiple_of(row * sr, sr), sr), :]


def _dispatch_kernel(dest_ref, pad_lo_ref, pad_hi_ref, x_ref, xs_ref, slab_s, zero_s, sems, zsem):
    tt = x_ref.shape[0]
    sr = zero_s.shape[0]
    i = pl.program_id(0)
    _store_slabs(slab_s, x_ref[...])

    def issue(r, carry):
        for k in range(TOP_K):
            pltpu.make_async_copy(_slab(slab_s, r, sr),
                                  _slab(xs_ref, dest_ref[(i * tt + r) * TOP_K + k], sr),
                                  sems.at[k]).start(priority=k % 2)
        return carry

    lax.fori_loop(0, tt, issue, 0, unroll=4)

    for k in range(TOP_K):
        pltpu.make_async_copy(slab_s, xs_ref.at[pl.ds(0, tt * sr), :], sems.at[k]).wait()

    @pl.when(i == pl.num_programs(0) - 1)
    def _():
        zero_s[...] = jnp.zeros(zero_s.shape, zero_s.dtype)

        def zero_copy(row):
            return pltpu.make_async_copy(zero_s, _slab(xs_ref, row, sr), zsem)

        def fill(e, carry):
            def one(row, c):
                zero_copy(row).start()
                return c
            return lax.fori_loop(pad_lo_ref[e], pad_hi_ref[e], one, carry)

        def drain(e, carry):
            def one(row, c):
                zero_copy(row).wait()
                return c
            return lax.fori_loop(pad_lo_ref[e], pad_hi_ref[e], one, carry)

        lax.fori_loop(0, pad_lo_ref.shape[0], fill, 0)
        lax.fori_loop(0, pad_lo_ref.shape[0], drain, 0)


def _dispatch(x, dest_flat, pad_lo, pad_hi, n_rows, *, tt=256):
    t, d = x.shape
    sr = d // LANES
    assert t % tt == 0 and d % LANES == 0 and sr % SUBLANES == 0
    grid_spec = pltpu.PrefetchScalarGridSpec(
        num_scalar_prefetch=3,
        grid=(t // tt,),
        in_specs=[pl.BlockSpec((tt, d), lambda i, *_: (i, 0))],
        out_specs=pl.BlockSpec(memory_space=pl.ANY),
        scratch_shapes=[pltpu.VMEM((tt * sr, LANES), F32),
                        pltpu.VMEM((sr, LANES), F32),
                        pltpu.SemaphoreType.DMA((TOP_K,)),
                        pltpu.SemaphoreType.DMA],
    )
    return pl.pallas_call(
        _dispatch_kernel,
        grid_spec=grid_spec,
        out_shape=jax.ShapeDtypeStruct((n_rows * sr, LANES), F32),
        compiler_params=_params(("arbitrary",)),
        name="moe_dispatch",
    )(dest_flat, pad_lo, pad_hi, x)


def _expert_kernel(block_e_ref, n_valid_ref, xs_ref, wgu_ref, bgu_ref, wd_ref, bd_ref, y_ref):
    del block_e_ref
    f = wd_ref.shape[0]
    tm = xs_ref.shape[0] * LANES // wgu_ref.shape[0]

    live = pl.program_id(0) < n_valid_ref[0]

    @pl.when(jnp.logical_not(live))
    def _():
        y_ref[...] = jnp.zeros(y_ref.shape, y_ref.dtype)

    @pl.when(live)
    def _():
        gu = jnp.dot(_load_slabs(xs_ref, tm).astype(BF16), wgu_ref[...],
                     preferred_element_type=F32) + bgu_ref[...]
        glu = jnp.minimum(gu[:, :f], SWIGLU_LIMIT)
        lin = jnp.clip(gu[:, f:], -SWIGLU_LIMIT, SWIGLU_LIMIT)
        h = glu * jax.nn.sigmoid(SWIGLU_ALPHA * glu) * (lin + 1.0)
        _store_slabs(y_ref, jnp.dot(h.astype(BF16), wd_ref[...],
                                    preferred_element_type=F32) + bd_ref[...])


def _experts(xs, block_e, n_valid, w_gu, b_gu, w_down, b_down, *, tm):
    n_exp, d, f2 = w_gu.shape
    f = f2 // 2
    sr = d // LANES
    nb = xs.shape[0] // (tm * sr)
    live = lambda i, be, nv: jnp.minimum(i, nv[0] - 1)
    grid_spec = pltpu.PrefetchScalarGridSpec(
        num_scalar_prefetch=2,
        grid=(nb,),
        in_specs=[pl.BlockSpec((tm * sr, LANES), lambda i, be, nv: (live(i, be, nv), 0)),
                  pl.BlockSpec((None, d, f2), lambda i, be, nv: (be[i], 0, 0)),
                  pl.BlockSpec((None, 1, f2), lambda i, be, nv: (be[i], 0, 0)),
                  pl.BlockSpec((None, f, d), lambda i, be, nv: (be[i], 0, 0)),
                  pl.BlockSpec((None, 1, d), lambda i, be, nv: (be[i], 0, 0))],
        out_specs=pl.BlockSpec((tm * sr, LANES), lambda i, be, nv: (i, 0)),
    )
    return pl.pallas_call(
        _expert_kernel,
        grid_spec=grid_spec,
        out_shape=jax.ShapeDtypeStruct(xs.shape, F32),
        compiler_params=_params(("arbitrary",)),
        name="moe_experts",
    )(block_e, n_valid, xs, w_gu, b_gu.reshape(n_exp, 1, f2), w_down,
      b_down.reshape(n_exp, 1, d))


def _combine_ln_kernel(dest_ref, y_ref, gate_ref, res_ref, g_ref, b_ref, o_ref, buf, sems,
                       *, alpha):
    tt = res_ref.shape[0]
    sr = res_ref.shape[1] // LANES
    i = pl.program_id(0)
    n = pl.num_programs(0)
    slot = i % 2

    def issue(step, s):
        def body(r, carry):
            for k in range(TOP_K):
                pltpu.make_async_copy(_slab(y_ref, dest_ref[(step * tt + r) * TOP_K + k], sr),
                                      _slab(buf.at[s, k], r, sr),
                                      sems.at[s, k]).start(priority=k % 2)
            return carry
        lax.fori_loop(0, tt, body, 0, unroll=4)

    @pl.when(i == 0)
    def _():
        issue(0, 0)

    @pl.when(i + 1 < n)
    def _():
        issue(i + 1, 1 - slot)

    for k in range(TOP_K):
        pltpu.make_async_copy(y_ref.at[pl.ds(0, tt * sr), :], buf.at[slot, k],
                              sems.at[slot, k]).wait()

    gate = gate_ref[...]
    ffn = _load_slabs(buf.at[slot, 0], tt) * gate[:, 0:1]
    for k in range(1, TOP_K):
        ffn = ffn + _load_slabs(buf.at[slot, k], tt) * gate[:, k:k + 1]
    o_ref[...] = _layer_norm(alpha * res_ref[...] + ffn, g_ref[...], b_ref[...])


def _combine_ln(y_rows, dest_flat, gates, res, g, b, *, alpha, tt=128):
    t, d = res.shape
    sr = d // LANES
    assert t % tt == 0
    grid_spec = pltpu.PrefetchScalarGridSpec(
        num_scalar_prefetch=1,
        grid=(t // tt,),
        in_specs=[pl.BlockSpec(memory_space=pl.ANY),
                  pl.BlockSpec((tt, TOP_K), lambda i, dest: (i, 0)),
                  pl.BlockSpec((tt, d), lambda i, dest: (i, 0)),
                  pl.BlockSpec((1, d), lambda i, dest: (0, 0)),
                  pl.BlockSpec((1, d), lambda i, dest: (0, 0))],
        out_specs=pl.BlockSpec((tt, d), lambda i, dest: (i, 0)),
        scratch_shapes=[pltpu.VMEM((2, TOP_K, tt * sr, LANES), F32),
                        pltpu.SemaphoreType.DMA((2, TOP_K))],
    )
    return pl.pallas_call(
        functools.partial(_combine_ln_kernel, alpha=alpha),
        grid_spec=grid_spec,
        out_shape=jax.ShapeDtypeStruct((t, d), F32),
        compiler_params=_params(("arbitrary",)),
        name="moe_combine_ln",
    )(dest_flat, y_rows, gates, res, g.reshape(1, d), b.reshape(1, d))


def _route(top_e, rank, counts, tm, n_blocks):
    n_exp = counts.shape[-1]
    counts = counts.reshape(n_exp).astype(jnp.int32)
    padded = (counts + tm - 1) // tm * tm
    pad_end = jnp.cumsum(padded)
    pad_start = pad_end - padded
    hit = top_e[:, :, None] == jnp.arange(n_exp, dtype=jnp.int32)
    dest = rank + jnp.sum(jnp.where(hit, pad_start, 0), axis=-1, dtype=jnp.int32)
    block_e = jnp.minimum(
        jnp.searchsorted(pad_end, jnp.arange(n_blocks, dtype=jnp.int32) * tm, side="right"),
        n_exp - 1).astype(jnp.int32)
    n_valid = (pad_end[-1] // tm).astype(jnp.int32).reshape(1)
    empty_lo = jnp.concatenate([pad_start + counts, pad_end[-1:]])
    empty_hi = jnp.concatenate([pad_end, jnp.full((1,), n_blocks * tm, jnp.int32)])
    return dest.reshape(-1), empty_lo, empty_hi, block_e, n_valid


def _moe_ln(x, top_e, gates, rank, counts, w_gu, b_gu, w_down, b_down, g, b, *, alpha, tm=256):
    t, _ = x.shape
    n_exp = w_gu.shape[0]
    n_blocks = t * TOP_K // tm + n_exp
    dest, pad_lo, pad_hi, block_e, n_valid = _route(top_e, rank, counts, tm, n_blocks)
    xs = _dispatch(x, dest, pad_lo, pad_hi, n_blocks * tm)
    y_rows = _experts(xs, block_e, n_valid, w_gu, b_gu, w_down, b_down, tm=tm)
    return _combine_ln(y_rows, dest, gates, x, g, b, alpha=alpha)


def _attn_kernel(*refs, scale):
    ng = len(DIL_GROUPS)
    q_refs = refs[0:ng]
    kc_refs = refs[ng:2 * ng]
    kp_refs = refs[2 * ng:3 * ng]
    vc_refs = refs[3 * ng:4 * ng]
    vp_refs = refs[4 * ng:5 * ng]
    o_ref = refs[5 * ng]
    og_s, lse_s = refs[5 * ng + 1:]
    blk = ATTN_BLK
    nblk = ATTN_SUPER // blk
    qi = lax.broadcasted_iota(jnp.int32, (nblk, blk, blk), 1)
    kj = lax.broadcasted_iota(jnp.int32, (nblk, blk, blk), 2)
    cur_ok = kj <= qi
    prev_ok = kj >= qi
    bidx = lax.broadcasted_iota(jnp.int32, (nblk, blk, blk), 0)
    seq_start = jnp.full((nblk, blk, blk), pl.program_id(1), jnp.int32) == 0
    qk = (((2,), (2,)), ((0,), (0,)))
    pv = (((2,), (1,)), ((0,), (0,)))

    for g, (_, dil) in enumerate(DIL_GROUPS):
        n_sub = nblk // dil
        span = n_sub * blk

        def blocks(ref, r, start_blk=0, count=n_sub, dil=dil):
            if dil == 1:
                rows = ref[start_blk * blk:(start_blk + count) * blk, :]
            else:
                rows = ref[pl.ds(r + dil * blk * start_blk, count * blk, stride=dil), :]
            return rows.reshape(count, blk, HEAD_DIM).astype(BF16)

        def with_prev(cur_ref, prev_ref, r, n_sub=n_sub):
            cur = blocks(cur_ref, r)
            last_of_prev = blocks(prev_ref, r, n_sub - 1, 1)
            if n_sub == 1:
                return cur, last_of_prev
            return cur, jnp.concatenate([last_of_prev, cur[:n_sub - 1]], axis=0)

        q, k_cur, k_prev, v_cur, v_prev = [], [], [], [], []
        for r in range(dil):
            q.append(blocks(q_refs[g], r))
            kc, kp = with_prev(kc_refs[g], kp_refs[g], r)
            vc, vp = with_prev(vc_refs[g], vp_refs[g], r)
            k_cur.append(kc)
            k_prev.append(kp)
            v_cur.append(vc)
            v_prev.append(vp)
        cat = lambda parts: parts[0] if len(parts) == 1 else jnp.concatenate(parts, axis=0)
        q, k_cur, k_prev, v_cur, v_prev = map(cat, (q, k_cur, k_prev, v_cur, v_prev))

        no_prev = seq_start & (bidx % n_sub == 0)
        s_cur = lax.dot_general(q, k_cur, qk, preferred_element_type=F32) * scale
        s_prev = lax.dot_general(q, k_prev, qk, preferred_element_type=F32) * scale
        s_cur = jnp.where(cur_ok, s_cur, -jnp.inf)
        s_prev = jnp.where(prev_ok & jnp.logical_not(no_prev), s_prev, -jnp.inf)
        m = jnp.maximum(jnp.max(s_cur, axis=-1, keepdims=True),
                        jnp.max(s_prev, axis=-1, keepdims=True))
        p_cur = jnp.exp(s_cur - m)
        p_prev = jnp.exp(s_prev - m)
        den = (jnp.sum(p_cur, axis=-1, keepdims=True)
               + jnp.sum(p_prev, axis=-1, keepdims=True))
        o = (lax.dot_general(p_cur.astype(BF16), v_cur, pv, preferred_element_type=F32)
             + lax.dot_general(p_prev.astype(BF16), v_prev, pv, preferred_element_type=F32)) / den
        lse = jnp.broadcast_to(m + jnp.log(den), (nblk, blk, LANES))
        for r in range(dil):
            o_r = o[r * n_sub:(r + 1) * n_sub].reshape(span, HEAD_DIM)
            lse_r = lse[r * n_sub:(r + 1) * n_sub].reshape(span, LANES)
            if dil == 1:
                og_s[g] = o_r
                lse_s[g] = lse_r
            else:
                og_s[g, pl.ds(r, span, stride=dil), :] = o_r
                lse_s[g, pl.ds(r, span, stride=dil), :] = lse_r

    top = lse_s[0]
    for g in range(1, ng):
        top = jnp.maximum(top, lse_s[g])
    num = jnp.zeros(top.shape, F32)
    den = jnp.zeros(top.shape, F32)
    for g in range(ng):
        w = jnp.exp(lse_s[g] - top)
        num = num + w * og_s[g]
        den = den + w
    o_ref[...] = (num / den).astype(o_ref.dtype)


def _attention(q, kv, *, batch, hpg):
    t = q.shape[0]
    seq = t // batch
    assert seq % ATTN_SUPER == 0
    ns = seq // ATTN_SUPER
    ng = len(DIL_GROUPS)
    nh = ng * hpg

    def cur(off):
        return lambda b, m, h, off=off: (b * ns + m, off + h)

    def prev(off):
        return lambda b, m, h, off=off: (b * ns + jnp.maximum(m - 1, 0), off + h)

    blk = lambda index_map: pl.BlockSpec((ATTN_SUPER, HEAD_DIM), index_map)
    in_specs = ([blk(cur(g * hpg)) for g in range(ng)]
                + [blk(cur(g * hpg)) for g in range(ng)]
                + [blk(prev(g * hpg)) for g in range(ng)]
                + [blk(cur(nh + g * hpg)) for g in range(ng)]
                + [blk(prev(nh + g * hpg)) for g in range(ng)])
    return pl.pallas_call(
        functools.partial(_attn_kernel, scale=HEAD_DIM ** -0.5),
        grid=(batch, ns, hpg),
        in_specs=in_specs,
        out_specs=pl.BlockSpec((ATTN_SUPER, HEAD_DIM), lambda b, m, h: (b * ns + m, h)),
        out_shape=jax.ShapeDtypeStruct((t, hpg * HEAD_DIM), BF16),
        scratch_shapes=[pltpu.VMEM((ng, ATTN_SUPER, HEAD_DIM), F32),
                        pltpu.VMEM((ng, ATTN_SUPER, LANES), F32)],
        compiler_params=_params(("parallel", "arbitrary", "arbitrary")),
        name="dilated_attn",
    )(*([q] * ng + [kv] * (4 * ng)))


def _rope_tables(positions):
    half = ROT_DIM // 2
    inv = jnp.power(ROPE_THETA, -jnp.arange(half, dtype=F32) * (2.0 / ROT_DIM))
    ang = positions.reshape(-1).astype(F32)[:, None] * inv
    cos, sin = jnp.cos(ang), jnp.sin(ang)
    rest = HEAD_DIM - ROT_DIM
    cos_t = jnp.concatenate([cos, cos, jnp.ones((ang.shape[0], rest), F32)], axis=1)
    sin_t = jnp.concatenate([-sin, sin, jnp.zeros((ang.shape[0], rest), F32)], axis=1)
    return cos_t, sin_t


def kernel(x, positions, a_w_in, a_conv_w, a_conv_b, a_w_rg, a_b_rg, a_w_ig, a_b_ig, a_lambda,
           a_w_out, kv_w, b_w_q, b_w_o, mix_ln_g, mix_ln_b, moe_w_router, moe_b_router,
           moe_w_gu, moe_b_gu, moe_w_down, moe_b_down, ffn_ln_g, ffn_ln_b):
    batch, seq, d = x.shape
    depth = mix_ln_g.shape[0]
    n_a = a_w_in.shape[0]
    c = a_w_in.shape[2] // 2
    attn_w = b_w_q.shape[2]
    hpg = b_w_o.shape[1] // HEAD_DIM
    alpha = (2 * depth) ** 0.25
    t = batch * seq

    x = x.reshape(t, d)
    cos_t, sin_t = _rope_tables(positions)
    head_tn = _pick(attn_w, (768, 512, 384, 256, 128))
    kv = None
    for layer in range(depth):
        if layer < n_a:
            proj = _proj(x, a_w_in[layer].astype(BF16), tn=_pick(c, (896, 640, 512, 384, 256, 128)),
                         epilogue="gelu", n_special_cols=c)
            mixed = _rglru(proj, a_conv_w[layer], a_conv_b[layer], a_w_rg[layer], a_b_rg[layer],
                           a_w_ig[layer], a_b_ig[layer], a_lambda[layer], batch=batch)
            w_out = a_w_out[layer]
        else:
            j = layer - n_a
            q = _proj(x, b_w_q[j].astype(BF16), tn=head_tn, epilogue="rope",
                      n_special_cols=attn_w, cos_t=cos_t, sin_t=sin_t)
            mixed = _attention(q, kv, batch=batch, hpg=hpg)
            w_out = b_w_o[j]
        x, top_e, gates, rank, counts = _mix_ln_router(
            mixed, w_out.astype(BF16), x, mix_ln_g[layer], mix_ln_b[layer], moe_w_router[layer],
            moe_b_router[layer], alpha=alpha)
        x = _moe_ln(x, top_e, gates, rank, counts, moe_w_gu[layer].astype(BF16), moe_b_gu[layer],
                    moe_w_down[layer].astype(BF16), moe_b_down[layer], ffn_ln_g[layer],
                    ffn_ln_b[layer], alpha=alpha)
        if layer == n_a - 1:
            kv = _proj(x, kv_w.astype(BF16), tn=head_tn, epilogue="rope",
                       n_special_cols=attn_w, cos_t=cos_t, sin_t=sin_t)
    return x.reshape(batch, seq, d)
```

```python
import functools

import jax
import jax.numpy as jnp
from jax import lax
from jax.experimental import pallas as pl
from jax.experimental.pallas import tpu as pltpu

F32 = jnp.float32
BF16 = jnp.bfloat16

CONV_W = 4
LRU_C = 8.0
HEAD_DIM = 128
ROT_DIM = HEAD_DIM // 4
ROPE_THETA = 500000.0
DIL_GROUPS = ((128, 1), (512, 4), (2048, 16))
ATTN_BLK = 128
ATTN_SUPER = 2048
TOP_K = 4
SWIGLU_LIMIT = 7.0
SWIGLU_ALPHA = 1.702
LN_EPS = 1e-5

LANES = 128
SUBLANES = 8
VMEM_LIMIT = 56 * 1024 * 1024


def _params(sem):
    return pltpu.CompilerParams(dimension_semantics=sem, vmem_limit_bytes=VMEM_LIMIT)


def _pick(n, candidates):
    for c in candidates:
        if n % c == 0:
            return c
    raise ValueError(f"no tile in {candidates} divides {n}")


def _layer_norm(y, g, b):
    mu = jnp.mean(y, axis=-1, keepdims=True)
    yc = y - mu
    var = jnp.mean(yc * yc, axis=-1, keepdims=True)
    return yc * lax.rsqrt(var + LN_EPS) * g + b


def _rope_chunk(c, cos_t, sin_t):
    lane = lax.broadcasted_iota(jnp.int32, c.shape, 1)
    partner = jnp.where(lane < ROT_DIM // 2,
                        pltpu.roll(c, LANES - ROT_DIM // 2, 1),
                        pltpu.roll(c, ROT_DIM // 2, 1))
    return c * cos_t + partner * sin_t


def _proj_kernel(x_ref, w_ref, *rest, epilogue):
    if epilogue == "rope":
        cos_ref, sin_ref, o_ref = rest
    else:
        (o_ref,) = rest
    acc = jnp.dot(x_ref[...].astype(BF16), w_ref[...], preferred_element_type=F32)
    if epilogue == "gelu":
        o_ref[...] = 0.5 * acc * (1.0 + jnp.tanh(
            0.7978845608028654 * (acc + 0.044715 * (acc * acc * acc))))
    elif epilogue == "rope":
        cos_t = cos_ref[...]
        sin_t = sin_ref[...]
        for h in range(acc.shape[1] // LANES):
            sl = slice(h * LANES, (h + 1) * LANES)
            o_ref[:, sl] = _rope_chunk(acc[:, sl], cos_t, sin_t)
    else:
        o_ref[...] = acc


def _proj(x, w, *, epilogue="none", cos_t=None, sin_t=None, tm=256):
    m, k = x.shape
    n = w.shape[1]
    assert m % tm == 0 and n % LANES == 0
    in_specs = [pl.BlockSpec((tm, k), lambda i: (i, 0)),
                pl.BlockSpec((k, n), lambda i: (0, 0))]
    args = [x, w]
    if epilogue == "rope":
        in_specs += [pl.BlockSpec((tm, LANES), lambda i: (i, 0))] * 2
        args += [cos_t, sin_t]
    return pl.pallas_call(
        functools.partial(_proj_kernel, epilogue=epilogue),
        grid=(m // tm,),
        in_specs=in_specs,
        out_specs=pl.BlockSpec((tm, n), lambda i: (i, 0)),
        out_shape=jax.ShapeDtypeStruct((m, n), F32),
        compiler_params=_params(("parallel",)),
        name=f"proj_{epilogue}",
    )(*args)


def _band_plan(c, bw, ct):
    plan = []
    for j in range(c // ct):
        c0 = j * ct
        lo = (c0 // bw) * bw
        hi = ((c0 + ct - 1) // bw + 1) * bw
        ks = (lo // LANES) * LANES
        kb = -(-(hi - ks) // LANES) * LANES
        ks = min(ks, c - kb)
        plan.append((ks, kb))
    return plan


def _band_weights(w_blocks, plan, ct):
    nb, bw, _ = w_blocks.shape
    c = nb * bw
    dense = jnp.zeros((nb, bw, nb, bw), w_blocks.dtype)
    dense = dense.at[jnp.arange(nb), :, jnp.arange(nb), :].set(w_blocks)
    dense = dense.reshape(c, c)
    kb_max = max(kb for _, kb in plan)
    slabs = []
    for j, (ks, kb) in enumerate(plan):
        slab = dense[ks:ks + kb, j * ct:(j + 1) * ct]
        slabs.append(jnp.pad(slab, ((0, kb_max - kb), (0, 0))))
    return jnp.stack(slabs).astype(BF16)


def _sigmoid(x):
    return 0.5 * jnp.tanh(0.5 * x) + 0.5


def _scan8(a, b):
    row = lax.broadcasted_iota(jnp.int32, a.shape, 0)
    for d in (1, 2, 4):
        a_prev = pltpu.roll(a, d, 0)
        b_prev = pltpu.roll(b, d, 0)
        keep = row >= d
        b = jnp.where(keep, a * b_prev + b, b)
        a = jnp.where(keep, a * a_prev, a)
    return a, b


def _rglru_kernel(gate_ref, u_ref, cw_ref, cb_ref, wrg_ref, brg_ref, wig_ref, big_ref,
                  lam_ref, o_ref, ubuf, uc_s, a_s, b_s, hcarry, *, plan, ct):
    ts = u_ref.shape[0]
    i = pl.program_id(1)

    @pl.when(i == 0)
    def _():
        ubuf[0:SUBLANES, :] = jnp.zeros((SUBLANES, ubuf.shape[1]), F32)
        hcarry[...] = jnp.zeros(hcarry.shape, F32)

    ubuf[SUBLANES:SUBLANES + ts, :] = u_ref[...]
    base = SUBLANES - (CONV_W - 1)
    conv = ubuf[base:base + ts, :] * cw_ref[0:1, :]
    for k in range(1, CONV_W):
        conv = conv + ubuf[base + k:base + k + ts, :] * cw_ref[k:k + 1, :]
    uc_s[...] = cb_ref[...] + conv
    ubuf[0:SUBLANES, :] = ubuf[ts:ts + SUBLANES, :]

    lam = lam_ref[...]
    softplus_neg_lam = jnp.maximum(-lam, 0.0) + jnp.log1p(jnp.exp(-jnp.abs(lam)))

    for j, (ks, kb) in enumerate(plan):
        cs = slice(j * ct, (j + 1) * ct)
        band = uc_s[:, ks:ks + kb].astype(BF16)
        r = _sigmoid(jnp.dot(band, wrg_ref[j, 0:kb, :], preferred_element_type=F32)
                     + brg_ref[:, cs])
        ig = _sigmoid(jnp.dot(band, wig_ref[j, 0:kb, :], preferred_element_type=F32)
                      + big_ref[:, cs])
        log_a = -LRU_C * r * softplus_neg_lam[:, cs]
        a = jnp.exp(log_a)
        a_s[...] = a
        b_s[...] = jnp.sqrt(-jnp.tanh(log_a) * (a * a + 1.0)) * ig * uc_s[:, cs]

        def group(g, carry, cs=cs):
            halves = []
            for s in range(2):
                rows = pl.ds(pl.multiple_of(g * 2 * SUBLANES + s * SUBLANES, SUBLANES), SUBLANES)
                a8, b8 = _scan8(a_s[rows, :], b_s[rows, :])
                h8 = b8 + a8 * carry
                carry = jnp.broadcast_to(h8[SUBLANES - 1:SUBLANES, :], h8.shape)
                halves.append(h8 * gate_ref[rows, cs])
            rows16 = pl.ds(pl.multiple_of(g * 2 * SUBLANES, 2 * SUBLANES), 2 * SUBLANES)
            o_ref[rows16, cs] = jnp.concatenate(halves, axis=0).astype(o_ref.dtype)
            return carry

        hcarry[:, cs] = lax.fori_loop(0, ts // (2 * SUBLANES), group, hcarry[:, cs])


def _rglru(gate, u, conv_w, conv_b, w_rg, b_rg, w_ig, b_ig, lam, *, batch, ts=256):
    t, c = u.shape
    nb, bw, _ = w_rg.shape
    ct = _pick(c, (384, 256, 128))
    plan = _band_plan(c, bw, ct)
    wrg = _band_weights(w_rg, plan, ct)
    wig = _band_weights(w_ig, plan, ct)
    seq = t // batch
    assert seq % ts == 0
    nt = seq // ts
    row = lambda v: v.reshape(1, c)
    vec = pl.BlockSpec((1, c), lambda b, i: (0, 0))
    slab = pl.BlockSpec(wrg.shape, lambda b, i: (0, 0, 0))
    return pl.pallas_call(
        functools.partial(_rglru_kernel, plan=plan, ct=ct),
        grid=(batch, nt),
        in_specs=[pl.BlockSpec((ts, c), lambda b, i: (b * nt + i, 0)),
                  pl.BlockSpec((ts, c), lambda b, i: (b * nt + i, 0)),
                  pl.BlockSpec((CONV_W, c), lambda b, i: (0, 0)),
                  vec, slab, vec, slab, vec, vec],
        out_specs=pl.BlockSpec((ts, c), lambda b, i: (b * nt + i, 0)),
        out_shape=jax.ShapeDtypeStruct((t, c), BF16),
        scratch_shapes=[pltpu.VMEM((ts + SUBLANES, c), F32),
                        pltpu.VMEM((ts, c), F32),
                        pltpu.VMEM((ts, ct), F32),
                        pltpu.VMEM((ts, ct), F32),
                        pltpu.VMEM((SUBLANES, c), F32)],
        compiler_params=_params(("arbitrary", "arbitrary")),
        name="rglru",
    )(gate, u, conv_w, row(conv_b), wrg, row(b_rg), wig, row(b_ig), row(lam))


def _mix_ln_router_kernel(a_ref, w_ref, res_ref, g_ref, b_ref, wr2_ref, br_ref,
                          xn_ref, e_ref, gate_ref, rank_ref, count_ref, run_s, *, alpha):
    tm = a_ref.shape[0]
    n_exp = br_ref.shape[1]

    @pl.when(pl.program_id(0) == 0)
    def _():
        run_s[...] = jnp.zeros(run_s.shape, F32)

    mix = jnp.dot(a_ref[...], w_ref[...], preferred_element_type=F32)
    xn = _layer_norm(alpha * res_ref[...] + mix, g_ref[...], b_ref[...])
    xn_ref[...] = xn

    x_hi = xn.astype(BF16)
    x_lo = (xn - x_hi.astype(F32)).astype(BF16)
    prod = jnp.dot(jnp.concatenate([x_hi, x_lo], axis=0), wr2_ref[...],
                   preferred_element_type=F32)
    logits = ((prod[:tm, n_exp:] + prod[tm:, :n_exp] + prod[tm:, n_exp:]) + prod[:tm, :n_exp]
              + br_ref[...])

    col = lax.broadcasted_iota(jnp.int32, logits.shape, 1).astype(F32)
    slot = lax.broadcasted_iota(jnp.int32, e_ref.shape, 1)
    e_out = jnp.zeros(e_ref.shape, F32)
    v_out = jnp.zeros(e_ref.shape, F32)
    picks = []
    for k in range(TOP_K):
        top = jnp.max(logits, axis=-1, keepdims=True)
        idx = jnp.min(jnp.where(logits == top, col, float(n_exp)), axis=-1, keepdims=True)
        picks.append(col == idx)
        logits = jnp.where(picks[k], -jnp.inf, logits)
        e_out = jnp.where(slot == k, idx, e_out)
        v_out = jnp.where(slot == k, top, v_out)
    p = jnp.exp(v_out - jnp.max(v_out, axis=-1, keepdims=True))
    e_ref[...] = e_out.astype(jnp.int32)
    gate_ref[...] = p / jnp.sum(p, axis=-1, keepdims=True)

    sel = jnp.zeros(logits.shape, F32)
    for k in range(TOP_K):
        sel = jnp.where(picks[k], 1.0, sel)
    earlier = (lax.broadcasted_iota(jnp.int32, (tm, tm), 1)
               < lax.broadcasted_iota(jnp.int32, (tm, tm), 0))
    before = jnp.dot(jnp.where(earlier, 1.0, 0.0).astype(BF16), sel.astype(BF16),
                     preferred_element_type=F32) + run_s[...]
    rank = jnp.zeros(e_ref.shape, F32)
    for k in range(TOP_K):
        rank_k = jnp.sum(jnp.where(picks[k], before, 0.0), axis=-1, keepdims=True)
        rank = jnp.where(slot == k, rank_k, rank)
    rank_ref[...] = rank.astype(jnp.int32)
    run_s[...] = run_s[...] + jnp.sum(sel, axis=0, keepdims=True)
    count_ref[...] = run_s[...]


def _mix_ln_router(a, w, res, g, b, w_router, b_router, *, alpha, tm=256):
    m, k = a.shape
    d = w.shape[1]
    n_exp = w_router.shape[1]
    assert m % tm == 0
    w_hi = w_router.astype(BF16)
    w_lo = (w_router - w_hi.astype(F32)).astype(BF16)
    full = lambda shape: pl.BlockSpec(shape, lambda i: (0,) * len(shape))
    return pl.pallas_call(
        functools.partial(_mix_ln_router_kernel, alpha=alpha),
        grid=(m // tm,),
        in_specs=[pl.BlockSpec((tm, k), lambda i: (i, 0)), full((k, d)),
                  pl.BlockSpec((tm, d), lambda i: (i, 0)), full((1, d)), full((1, d)),
                  full((d, 2 * n_exp)), full((1, n_exp))],
        out_specs=[pl.BlockSpec((tm, d), lambda i: (i, 0)),
                   pl.BlockSpec((tm, TOP_K), lambda i: (i, 0)),
                   pl.BlockSpec((tm, TOP_K), lambda i: (i, 0)),
                   pl.BlockSpec((tm, TOP_K), lambda i: (i, 0)),
                   full((1, n_exp))],
        out_shape=[jax.ShapeDtypeStruct((m, d), F32),
                   jax.ShapeDtypeStruct((m, TOP_K), jnp.int32),
                   jax.ShapeDtypeStruct((m, TOP_K), F32),
                   jax.ShapeDtypeStruct((m, TOP_K), jnp.int32),
                   jax.ShapeDtypeStruct((1, n_exp), F32)],
        scratch_shapes=[pltpu.VMEM((1, n_exp), F32)],
        compiler_params=_params(("arbitrary",)),
        name="mix_ln_router",
    )(a, w, res, g.reshape(1, d), b.reshape(1, d), jnp.concatenate([w_hi, w_lo], axis=1),
      b_router.reshape(1, n_exp))


def _dispatch_kernel(dest_ref, empty_lo_ref, empty_hi_ref, x_ref, xs_ref, zero_s, sems, zsem):
    tt = x_ref.shape[0]
    i = pl.program_id(0)

    def issue(g, carry):
        row0 = pl.multiple_of(g * SUBLANES, SUBLANES)
        slot0 = (i * tt + row0) * TOP_K
        for j in range(SUBLANES):
            for k in range(TOP_K):
                pltpu.make_async_copy(
                    x_ref.at[pl.ds(row0 + j, 1), :],
                    xs_ref.at[pl.ds(dest_ref[slot0 + (j * TOP_K + k)], 1), :],
                    sems.at[k]).start(priority=k % 2)
        return carry

    lax.fori_loop(0, tt // SUBLANES, issue, 0)

    for k in range(TOP_K):
        pltpu.make_async_copy(x_ref, xs_ref.at[pl.ds(0, tt), :], sems.at[k]).wait()

    @pl.when(i == pl.num_programs(0) - 1)
    def _():
        zero_s[...] = jnp.zeros(zero_s.shape, zero_s.dtype)

        def zero_copy(row):
            return pltpu.make_async_copy(zero_s, xs_ref.at[pl.ds(row, 1), :], zsem)

        def fill(e, carry):
            def one(row, c):
                zero_copy(row).start()
                return c
            return lax.fori_loop(empty_lo_ref[e], empty_hi_ref[e], one, carry)

        def drain(e, carry):
            def one(row, c):
                zero_copy(row).wait()
                return c
            return lax.fori_loop(empty_lo_ref[e], empty_hi_ref[e], one, carry)

        lax.fori_loop(0, empty_lo_ref.shape[0], fill, 0)
        lax.fori_loop(0, empty_lo_ref.shape[0], drain, 0)


def _dispatch(x, dest_flat, empty_lo, empty_hi, n_rows, *, tt=256):
    t, d = x.shape
    assert t % tt == 0
    grid_spec = pltpu.PrefetchScalarGridSpec(
        num_scalar_prefetch=3,
        grid=(t // tt,),
        in_specs=[pl.BlockSpec((tt, d), lambda i, *_: (i, 0))],
        out_specs=pl.BlockSpec(memory_space=pl.ANY),
        scratch_shapes=[pltpu.VMEM((1, d), F32),
                        pltpu.SemaphoreType.DMA((TOP_K,)),
                        pltpu.SemaphoreType.DMA],
    )
    return pl.pallas_call(
        _dispatch_kernel,
        grid_spec=grid_spec,
        out_shape=jax.ShapeDtypeStruct((n_rows, d), F32),
        compiler_params=_params(("arbitrary",)),
        name="moe_dispatch",
    )(dest_flat, empty_lo, empty_hi, x)


def _expert_kernel(block_e_ref, n_valid_ref, xs_ref, wgu_ref, bgu_ref, wd_ref, bd_ref, y_ref):
    del block_e_ref
    f = wd_ref.shape[0]

    live = pl.program_id(0) < n_valid_ref[0]

    @pl.when(jnp.logical_not(live))
    def _():
        y_ref[...] = jnp.zeros(y_ref.shape, y_ref.dtype)

    @pl.when(live)
    def _():
        gu = jnp.dot(xs_ref[...].astype(BF16), wgu_ref[...],
                     preferred_element_type=F32) + bgu_ref[...]
        glu = jnp.minimum(gu[:, :f], SWIGLU_LIMIT)
        lin = jnp.clip(gu[:, f:], -SWIGLU_LIMIT, SWIGLU_LIMIT)
        h = glu * jax.nn.sigmoid(SWIGLU_ALPHA * glu) * (lin + 1.0)
        y_ref[...] = jnp.dot(h.astype(BF16), wd_ref[...],
                             preferred_element_type=F32) + bd_ref[...]


def _experts(xs, block_e, n_valid, w_gu, b_gu, w_down, b_down, *, tm):
    r, d = xs.shape
    n_exp, _, f2 = w_gu.shape
    f = f2 // 2
    nb = r // tm
    live = lambda i, be, nv: jnp.minimum(i, nv[0] - 1)
    grid_spec = pltpu.PrefetchScalarGridSpec(
        num_scalar_prefetch=2,
        grid=(nb,),
        in_specs=[pl.BlockSpec((tm, d), lambda i, be, nv: (live(i, be, nv), 0)),
                  pl.BlockSpec((None, d, f2), lambda i, be, nv: (be[i], 0, 0)),
                  pl.BlockSpec((None, 1, f2), lambda i, be, nv: (be[i], 0, 0)),
                  pl.BlockSpec((None, f, d), lambda i, be, nv: (be[i], 0, 0)),
                  pl.BlockSpec((None, 1, d), lambda i, be, nv: (be[i], 0, 0))],
        out_specs=pl.BlockSpec((tm, d), lambda i, be, nv: (i, 0)),
    )
    return pl.pallas_call(
        _expert_kernel,
        grid_spec=grid_spec,
        out_shape=jax.ShapeDtypeStruct((r, d), F32),
        compiler_params=_params(("arbitrary",)),
        name="moe_experts",
    )(block_e, n_valid, xs, w_gu, b_gu.reshape(n_exp, 1, f2), w_down,
      b_down.reshape(n_exp, 1, d))


def _combine_ln_kernel(dest_ref, y_ref, gate_ref, res_ref, g_ref, b_ref, o_ref, buf, sems,
                       *, alpha):
    tt = res_ref.shape[0]
    i = pl.program_id(0)
    n = pl.num_programs(0)
    slot = i % 2

    def issue(step, s):
        def body(g, carry):
            row0 = pl.multiple_of(g * SUBLANES, SUBLANES)
            slot0 = (step * tt + row0) * TOP_K
            for j in range(SUBLANES):
                for k in range(TOP_K):
                    pltpu.make_async_copy(
                        y_ref.at[pl.ds(dest_ref[slot0 + (j * TOP_K + k)], 1), :],
                        buf.at[s, k, pl.ds(row0 + j, 1), :],
                        sems.at[s, k]).start(priority=k % 2)
            return carry
        lax.fori_loop(0, tt // SUBLANES, body, 0)

    @pl.when(i == 0)
    def _():
        issue(0, 0)

    @pl.when(i + 1 < n)
    def _():
        issue(i + 1, 1 - slot)

    for k in range(TOP_K):
        pltpu.make_async_copy(y_ref.at[pl.ds(0, tt), :], buf.at[slot, k], sems.at[slot, k]).wait()

    gate = gate_ref[...]
    ffn = buf[slot, 0] * gate[:, 0:1]
    for k in range(1, TOP_K):
        ffn = ffn + buf[slot, k] * gate[:, k:k + 1]
    o_ref[...] = _layer_norm(alpha * res_ref[...] + ffn, g_ref[...], b_ref[...])


def _combine_ln(y_rows, dest_flat, gates, res, g, b, *, alpha, tt=128):
    t, d = res.shape
    assert t % tt == 0
    grid_spec = pltpu.PrefetchScalarGridSpec(
        num_scalar_prefetch=1,
        grid=(t // tt,),
        in_specs=[pl.BlockSpec(memory_space=pl.ANY),
                  pl.BlockSpec((tt, TOP_K), lambda i, dest: (i, 0)),
                  pl.BlockSpec((tt, d), lambda i, dest: (i, 0)),
                  pl.BlockSpec((1, d), lambda i, dest: (0, 0)),
                  pl.BlockSpec((1, d), lambda i, dest: (0, 0))],
        out_specs=pl.BlockSpec((tt, d), lambda i, dest: (i, 0)),
        scratch_shapes=[pltpu.VMEM((2, TOP_K, tt, d), F32),
                        pltpu.SemaphoreType.DMA((2, TOP_K))],
    )
    return pl.pallas_call(
        functools.partial(_combine_ln_kernel, alpha=alpha),
        grid_spec=grid_spec,
        out_shape=jax.ShapeDtypeStruct((t, d), F32),
        compiler_params=_params(("arbitrary",)),
        name="moe_combine_ln",
    )(dest_flat, y_rows, gates, res, g.reshape(1, d), b.reshape(1, d))


def _route(top_e, rank, counts, tm, n_blocks):
    n_exp = counts.shape[-1]
    counts = counts.reshape(n_exp).astype(jnp.int32)
    padded = (counts + tm - 1) // tm * tm
    pad_end = jnp.cumsum(padded)
    pad_start = pad_end - padded
    hit = top_e[:, :, None] == jnp.arange(n_exp, dtype=jnp.int32)
    dest = rank + jnp.sum(jnp.where(hit, pad_start, 0), axis=-1, dtype=jnp.int32)
    first_row = jnp.arange(n_blocks, dtype=jnp.int32) * tm
    block_e = jnp.minimum(jnp.sum(pad_end[None, :] <= first_row[:, None], axis=1, dtype=jnp.int32),
                          n_exp - 1)
    n_valid = (pad_end[-1] // tm).astype(jnp.int32).reshape(1)
    empty_lo = jnp.concatenate([pad_start + counts, pad_end[-1:]])
    empty_hi = jnp.concatenate([pad_end, jnp.full((1,), n_blocks * tm, jnp.int32)])
    return dest.reshape(-1), empty_lo, empty_hi, block_e, n_valid


def _moe_ln(x, top_e, gates, rank, counts, w_gu, b_gu, w_down, b_down, g, b, *, alpha, tm=256):
    t, _ = x.shape
    n_exp = w_gu.shape[0]
    n_blocks = t * TOP_K // tm + n_exp
    dest, pad_lo, pad_hi, block_e, n_valid = _route(top_e, rank, counts, tm, n_blocks)
    xs = _dispatch(x, dest, pad_lo, pad_hi, n_blocks * tm)
    y_rows = _experts(xs, block_e, n_valid, w_gu, b_gu, w_down, b_down, tm=tm)
    return _combine_ln(y_rows, dest, gates, x, g, b, alpha=alpha)


def _attn_kernel(*refs, scale):
    ng = len(DIL_GROUPS)
    q_refs = refs[0:ng]
    kc_refs = refs[ng:2 * ng]
    kp_refs = refs[2 * ng:3 * ng]
    vc_refs = refs[3 * ng:4 * ng]
    vp_refs = refs[4 * ng:5 * ng]
    o_ref = refs[5 * ng]
    og_s, lse_s = refs[5 * ng + 1:]
    blk = ATTN_BLK
    nblk = ATTN_SUPER // blk
    qi = lax.broadcasted_iota(jnp.int32, (nblk, blk, blk), 1)
    kj = lax.broadcasted_iota(jnp.int32, (nblk, blk, blk), 2)
    cur_ok = kj <= qi
    prev_ok = kj >= qi
    bidx = lax.broadcasted_iota(jnp.int32, (nblk, blk, blk), 0)
    seq_start = jnp.full((nblk, blk, blk), pl.program_id(1), jnp.int32) == 0
    qk = (((2,), (2,)), ((0,), (0,)))
    pv = (((2,), (1,)), ((0,), (0,)))

    for g, (_, dil) in enumerate(DIL_GROUPS):
        n_sub = nblk // dil
        span = n_sub * blk

        def blocks(ref, r, start_blk=0, count=n_sub, dil=dil):
            if dil == 1:
                rows = ref[start_blk * blk:(start_blk + count) * blk, :]
            else:
                rows = ref[pl.ds(r + dil * blk * start_blk, count * blk, stride=dil), :]
            return rows.reshape(count, blk, HEAD_DIM).astype(BF16)

        def with_prev(cur_ref, prev_ref, r, n_sub=n_sub):
            cur = blocks(cur_ref, r)
            last_of_prev = blocks(prev_ref, r, n_sub - 1, 1)
            if n_sub == 1:
                return cur, last_of_prev
            return cur, jnp.concatenate([last_of_prev, cur[:n_sub - 1]], axis=0)

        q, k_cur, k_prev, v_cur, v_prev = [], [], [], [], []
        for r in range(dil):
            q.append(blocks(q_refs[g], r))
            kc, kp = with_prev(kc_refs[g], kp_refs[g], r)
            vc, vp = with_prev(vc_refs[g], vp_refs[g], r)
            k_cur.append(kc)
            k_prev.append(kp)
            v_cur.append(vc)
            v_prev.append(vp)
        cat = lambda parts: parts[0] if len(parts) == 1 else jnp.concatenate(parts, axis=0)
        q, k_cur, k_prev, v_cur, v_prev = map(cat, (q, k_cur, k_prev, v_cur, v_prev))

        no_prev = seq_start & (bidx % n_sub == 0)
        s_cur = lax.dot_general(q, k_cur, qk, preferred_element_type=F32) * scale
        s_prev = lax.dot_general(q, k_prev, qk, preferred_element_type=F32) * scale
        s_cur = jnp.where(cur_ok, s_cur, -jnp.inf)
        s_prev = jnp.where(prev_ok & jnp.logical_not(no_prev), s_prev, -jnp.inf)
        m = jnp.maximum(jnp.max(s_cur, axis=-1, keepdims=True),
                        jnp.max(s_prev, axis=-1, keepdims=True))
        p_cur = jnp.exp(s_cur - m)
        p_prev = jnp.exp(s_prev - m)
        den = (jnp.sum(p_cur, axis=-1, keepdims=True)
               + jnp.sum(p_prev, axis=-1, keepdims=True))
        o = (lax.dot_general(p_cur.astype(BF16), v_cur, pv, preferred_element_type=F32)
             + lax.dot_general(p_prev.astype(BF16), v_prev, pv, preferred_element_type=F32)) / den
        lse = jnp.broadcast_to(m + jnp.log(den), (nblk, blk, LANES))
        for r in range(dil):
            o_r = o[r * n_sub:(r + 1) * n_sub].reshape(span, HEAD_DIM)
            lse_r = lse[r * n_sub:(r + 1) * n_sub].reshape(span, LANES)
            if dil == 1:
                og_s[g] = o_r
                lse_s[g] = lse_r
            else:
                og_s[g, pl.ds(r, span, stride=dil), :] = o_r
                lse_s[g, pl.ds(r, span, stride=dil), :] = lse_r

    top = lse_s[0]
    for g in range(1, ng):
        top = jnp.maximum(top, lse_s[g])
    num = jnp.zeros(top.shape, F32)
    den = jnp.zeros(top.shape, F32)
    for g in range(ng):
        w = jnp.exp(lse_s[g] - top)
        num = num + w * og_s[g]
        den = den + w
    o_ref[...] = (num / den).astype(o_ref.dtype)


def _attention(q, k, v, *, batch, hpg):
    t = q.shape[0]
    seq = t // batch
    assert seq % ATTN_SUPER == 0
    ns = seq // ATTN_SUPER
    ng = len(DIL_GROUPS)

    def cur(off):
        return lambda b, m, h, off=off: (b * ns + m, off + h)

    def prev(off):
        return lambda b, m, h, off=off: (b * ns + jnp.maximum(m - 1, 0), off + h)

    blk = lambda index_map: pl.BlockSpec((ATTN_SUPER, HEAD_DIM), index_map)
    in_specs = ([blk(cur(g * hpg)) for g in range(ng)]
                + [blk(cur(g * hpg)) for g in range(ng)]
                + [blk(prev(g * hpg)) for g in range(ng)]
                + [blk(cur(g * hpg)) for g in range(ng)]
                + [blk(prev(g * hpg)) for g in range(ng)])
    return pl.pallas_call(
        functools.partial(_attn_kernel, scale=HEAD_DIM ** -0.5),
        grid=(batch, ns, hpg),
        in_specs=in_specs,
        out_specs=pl.BlockSpec((ATTN_SUPER, HEAD_DIM), lambda b, m, h: (b * ns + m, h)),
        out_shape=jax.ShapeDtypeStruct((t, hpg * HEAD_DIM), BF16),
        scratch_shapes=[pltpu.VMEM((ng, ATTN_SUPER, HEAD_DIM), F32),
                        pltpu.VMEM((ng, ATTN_SUPER, LANES), F32)],
        compiler_params=_params(("parallel", "arbitrary", "arbitrary")),
        name="dilated_attn",
    )(*([q] * ng + [k] * (2 * ng) + [v] * (2 * ng)))


def _rope_tables(positions):
    half = ROT_DIM // 2
    inv = jnp.power(ROPE_THETA, -jnp.arange(half, dtype=F32) * (2.0 / ROT_DIM))
    ang = positions.reshape(-1).astype(F32)[:, None] * inv
    cos, sin = jnp.cos(ang), jnp.sin(ang)
    rest = HEAD_DIM - ROT_DIM
    cos_t = jnp.concatenate([cos, cos, jnp.ones((ang.shape[0], rest), F32)], axis=1)
    sin_t = jnp.concatenate([-sin, sin, jnp.zeros((ang.shape[0], rest), F32)], axis=1)
    return cos_t, sin_t


def kernel(x, positions, a_w_in, a_conv_w, a_conv_b, a_w_rg, a_b_rg, a_w_ig, a_b_ig, a_lambda,
           a_w_out, kv_w, b_w_q, b_w_o, mix_ln_g, mix_ln_b, moe_w_router, moe_b_router,
           moe_w_gu, moe_b_gu, moe_w_down, moe_b_down, ffn_ln_g, ffn_ln_b):
    batch, seq, d = x.shape
    depth = mix_ln_g.shape[0]
    n_a = a_w_in.shape[0]
    c = a_w_in.shape[2] // 2
    attn_w = b_w_q.shape[2]
    hpg = b_w_o.shape[1] // HEAD_DIM
    alpha = (2 * depth) ** 0.25
    t = batch * seq

    x = x.reshape(t, d)
    cos_t, sin_t = _rope_tables(positions)
    k_sh = v_sh = None
    for layer in range(depth):
        if layer < n_a:
            gate = _proj(x, a_w_in[layer][:, :c].astype(BF16), epilogue="gelu")
            u = _proj(x, a_w_in[layer][:, c:].astype(BF16))
            mixed = _rglru(gate, u, a_conv_w[layer], a_conv_b[layer], a_w_rg[layer], a_b_rg[layer],
                           a_w_ig[layer], a_b_ig[layer], a_lambda[layer], batch=batch)
            w_out = a_w_out[layer]
        else:
            j = layer - n_a
            q = _proj(x, b_w_q[j].astype(BF16), epilogue="rope", cos_t=cos_t, sin_t=sin_t)
            mixed = _attention(q, k_sh, v_sh, batch=batch, hpg=hpg)
            w_out = b_w_o[j]
        x, top_e, gates, rank, counts = _mix_ln_router(
            mixed, w_out.astype(BF16), x, mix_ln_g[layer], mix_ln_b[layer], moe_w_router[layer],
            moe_b_router[layer], alpha=alpha)
        x = _moe_ln(x, top_e, gates, rank, counts, moe_w_gu[layer].astype(BF16), moe_b_gu[layer],
                    moe_w_down[layer].astype(BF16), moe_b_down[layer], ffn_ln_g[layer],
                    ffn_ln_b[layer], alpha=alpha)
        if layer == n_a - 1:
            k_sh = _proj(x, kv_w[:, :attn_w].astype(BF16), epilogue="rope",
                         cos_t=cos_t, sin_t=sin_t)
            v_sh = _proj(x, kv_w[:, attn_w:].astype(BF16))
    return x.reshape(batch, seq, d)
```

```python
import functools

import jax
import jax.numpy as jnp
from jax import lax
from jax.experimental import pallas as pl
from jax.experimental.pallas import tpu as pltpu

F32 = jnp.float32
BF16 = jnp.bfloat16

CONV_W = 4
LRU_C = 8.0
HEAD_DIM = 128
ROT_DIM = HEAD_DIM // 4
ROPE_THETA = 500000.0
DIL_GROUPS = ((128, 1), (512, 4), (2048, 16))
ATTN_BLK = 128
ATTN_SUPER = 2048
TOP_K = 4
SWIGLU_LIMIT = 7.0
SWIGLU_ALPHA = 1.702
LN_EPS = 1e-5

LANES = 128
SUBLANES = 8
VMEM_LIMIT = 56 * 1024 * 1024


def _params(sem):
    return pltpu.CompilerParams(dimension_semantics=sem, vmem_limit_bytes=VMEM_LIMIT)


def _pick(n, candidates):
    for c in candidates:
        if n % c == 0:
            return c
    raise ValueError(f"no tile in {candidates} divides {n}")


def _layer_norm(y, g, b):
    mu = jnp.mean(y, axis=-1, keepdims=True)
    yc = y - mu
    var = jnp.mean(yc * yc, axis=-1, keepdims=True)
    return yc * lax.rsqrt(var + LN_EPS) * g + b


def _rope_chunk(c, cos_t, sin_t):
    lane = lax.broadcasted_iota(jnp.int32, c.shape, 1)
    partner = jnp.where(lane < ROT_DIM // 2,
                        pltpu.roll(c, LANES - ROT_DIM // 2, 1),
                        pltpu.roll(c, ROT_DIM // 2, 1))
    return c * cos_t + partner * sin_t


def _proj_kernel(x_ref, w_ref, *rest, epilogue):
    if epilogue == "rope":
        cos_ref, sin_ref, o_ref = rest
    else:
        (o_ref,) = rest
    acc = jnp.dot(x_ref[...].astype(BF16), w_ref[...], preferred_element_type=F32)
    if epilogue == "gelu":
        o_ref[...] = 0.5 * acc * (1.0 + jnp.tanh(
            0.7978845608028654 * (acc + 0.044715 * (acc * acc * acc))))
    elif epilogue == "rope":
        cos_t = cos_ref[...]
        sin_t = sin_ref[...]
        for h in range(acc.shape[1] // LANES):
            sl = slice(h * LANES, (h + 1) * LANES)
            o_ref[:, sl] = _rope_chunk(acc[:, sl], cos_t, sin_t)
    else:
        o_ref[...] = acc


def _proj(x, w, *, epilogue="none", cos_t=None, sin_t=None, tm=256):
    m, k = x.shape
    n = w.shape[1]
    assert m % tm == 0 and n % LANES == 0
    in_specs = [pl.BlockSpec((tm, k), lambda i: (i, 0)),
                pl.BlockSpec((k, n), lambda i: (0, 0))]
    args = [x, w]
    if epilogue == "rope":
        in_specs += [pl.BlockSpec((tm, LANES), lambda i: (i, 0))] * 2
        args += [cos_t, sin_t]
    return pl.pallas_call(
        functools.partial(_proj_kernel, epilogue=epilogue),
        grid=(m // tm,),
        in_specs=in_specs,
        out_specs=pl.BlockSpec((tm, n), lambda i: (i, 0)),
        out_shape=jax.ShapeDtypeStruct((m, n), F32),
        compiler_params=_params(("parallel",)),
        name=f"proj_{epilogue}",
    )(*args)


def _band_plan(c, bw, ct):
    plan = []
    for j in range(c // ct):
        c0 = j * ct
        lo = (c0 // bw) * bw
        hi = ((c0 + ct - 1) // bw + 1) * bw
        ks = (lo // LANES) * LANES
        kb = -(-(hi - ks) // LANES) * LANES
        ks = min(ks, c - kb)
        plan.append((ks, kb))
    return plan


def _band_weights(w_blocks, plan, ct):
    nb, bw, _ = w_blocks.shape
    c = nb * bw
    dense = jnp.zeros((nb, bw, nb, bw), w_blocks.dtype)
    dense = dense.at[jnp.arange(nb), :, jnp.arange(nb), :].set(w_blocks)
    dense = dense.reshape(c, c)
    kb_max = max(kb for _, kb in plan)
    slabs = []
    for j, (ks, kb) in enumerate(plan):
        slab = dense[ks:ks + kb, j * ct:(j + 1) * ct]
        slabs.append(jnp.pad(slab, ((0, kb_max - kb), (0, 0))))
    return jnp.stack(slabs).astype(BF16)


def _sigmoid(x):
    return 0.5 * jnp.tanh(0.5 * x) + 0.5


def _scan8(a, b):
    row = lax.broadcasted_iota(jnp.int32, a.shape, 0)
    for d in (1, 2, 4):
        a_prev = pltpu.roll(a, d, 0)
        b_prev = pltpu.roll(b, d, 0)
        keep = row >= d
        b = jnp.where(keep, a * b_prev + b, b)
        a = jnp.where(keep, a * a_prev, a)
    return a, b


def _rglru_kernel(gate_ref, u_ref, cw_ref, cb_ref, wrg_ref, brg_ref, wig_ref, big_ref,
                  lam_ref, o_ref, ubuf, uc_s, hcarry, *, plan, ct):
    ts = u_ref.shape[0]
    i = pl.program_id(1)

    @pl.when(i == 0)
    def _():
        ubuf[0:SUBLANES, :] = jnp.zeros((SUBLANES, ubuf.shape[1]), F32)
        hcarry[...] = jnp.zeros(hcarry.shape, F32)

    ubuf[SUBLANES:SUBLANES + ts, :] = u_ref[...]
    base = SUBLANES - (CONV_W - 1)
    conv = ubuf[base:base + ts, :] * cw_ref[0:1, :]
    for k in range(1, CONV_W):
        conv = conv + ubuf[base + k:base + k + ts, :] * cw_ref[k:k + 1, :]
    uc_s[...] = cb_ref[...] + conv
    ubuf[0:SUBLANES, :] = ubuf[ts:ts + SUBLANES, :]

    lam = lam_ref[...]
    softplus_neg_lam = jnp.maximum(-lam, 0.0) + jnp.log1p(jnp.exp(-jnp.abs(lam)))

    for j, (ks, kb) in enumerate(plan):
        cs = slice(j * ct, (j + 1) * ct)
        band = uc_s[:, ks:ks + kb].astype(BF16)
        r_pre = jnp.dot(band, wrg_ref[j, 0:kb, :], preferred_element_type=F32) + brg_ref[:, cs]
        i_pre = jnp.dot(band, wig_ref[j, 0:kb, :], preferred_element_type=F32) + big_ref[:, cs]
        carry = hcarry[:, cs]
        for c0 in range(0, ts, SCAN_ROWS):
            rs = slice(c0, c0 + SCAN_ROWS)
            log_a = -LRU_C * _sigmoid(r_pre[rs]) * softplus_neg_lam[:, cs]
            a = jnp.exp(log_a)
            b = (jnp.sqrt(-jnp.tanh(log_a) * (a * a + 1.0)) * _sigmoid(i_pre[rs])
                 * uc_s[rs, cs])
            outs = []
            for s0 in range(0, SCAN_ROWS, SUBLANES):
                a8, b8 = _scan8(a[s0:s0 + SUBLANES], b[s0:s0 + SUBLANES])
                h8 = b8 + a8 * carry
                carry = jnp.broadcast_to(h8[SUBLANES - 1:SUBLANES, :], h8.shape)
                outs.append(h8 * gate_ref[c0 + s0:c0 + s0 + SUBLANES, cs])
            o_ref[rs, cs] = jnp.concatenate(outs, axis=0).astype(o_ref.dtype)
        hcarry[:, cs] = carry


SCAN_ROWS = 64


def _rglru(gate, u, conv_w, conv_b, w_rg, b_rg, w_ig, b_ig, lam, *, batch, ts=256):
    t, c = u.shape
    nb, bw, _ = w_rg.shape
    ct = _pick(c, (384, 256, 128))
    plan = _band_plan(c, bw, ct)
    wrg = _band_weights(w_rg, plan, ct)
    wig = _band_weights(w_ig, plan, ct)
    seq = t // batch
    assert seq % ts == 0
    nt = seq // ts
    row = lambda v: v.reshape(1, c)
    vec = pl.BlockSpec((1, c), lambda b, i: (0, 0))
    slab = pl.BlockSpec(wrg.shape, lambda b, i: (0, 0, 0))
    return pl.pallas_call(
        functools.partial(_rglru_kernel, plan=plan, ct=ct),
        grid=(batch, nt),
        in_specs=[pl.BlockSpec((ts, c), lambda b, i: (b * nt + i, 0)),
                  pl.BlockSpec((ts, c), lambda b, i: (b * nt + i, 0)),
                  pl.BlockSpec((CONV_W, c), lambda b, i: (0, 0)),
                  vec, slab, vec, slab, vec, vec],
        out_specs=pl.BlockSpec((ts, c), lambda b, i: (b * nt + i, 0)),
        out_shape=jax.ShapeDtypeStruct((t, c), BF16),
        scratch_shapes=[pltpu.VMEM((ts + SUBLANES, c), F32),
                        pltpu.VMEM((ts, c), F32),
                        pltpu.VMEM((SUBLANES, c), F32)],
        compiler_params=_params(("arbitrary", "arbitrary")),
        name="rglru",
    )(gate, u, conv_w, row(conv_b), wrg, row(b_rg), wig, row(b_ig), row(lam))


def _mix_ln_router_kernel(a_ref, w_ref, res_ref, g_ref, b_ref, wr2_ref, br_ref,
                          xn_ref, e_ref, gate_ref, rank_ref, count_ref, run_s, *, alpha):
    tm = a_ref.shape[0]
    n_exp = br_ref.shape[1]

    @pl.when(pl.program_id(0) == 0)
    def _():
        run_s[...] = jnp.zeros(run_s.shape, F32)

    mix = jnp.dot(a_ref[...], w_ref[...], preferred_element_type=F32)
    xn = _layer_norm(alpha * res_ref[...] + mix, g_ref[...], b_ref[...])
    xn_ref[...] = xn

    x_hi = xn.astype(BF16)
    x_lo = (xn - x_hi.astype(F32)).astype(BF16)
    prod = jnp.dot(jnp.concatenate([x_hi, x_lo], axis=0), wr2_ref[...],
                   preferred_element_type=F32)
    logits = ((prod[:tm, n_exp:] + prod[tm:, :n_exp] + prod[tm:, n_exp:]) + prod[:tm, :n_exp]
              + br_ref[...])

    col = lax.broadcasted_iota(jnp.int32, logits.shape, 1).astype(F32)
    slot = lax.broadcasted_iota(jnp.int32, e_ref.shape, 1)
    e_out = jnp.zeros(e_ref.shape, F32)
    v_out = jnp.zeros(e_ref.shape, F32)
    picks = []
    for k in range(TOP_K):
        top = jnp.max(logits, axis=-1, keepdims=True)
        idx = jnp.min(jnp.where(logits == top, col, float(n_exp)), axis=-1, keepdims=True)
        picks.append(col == idx)
        logits = jnp.where(picks[k], -jnp.inf, logits)
        e_out = jnp.where(slot == k, idx, e_out)
        v_out = jnp.where(slot == k, top, v_out)
    p = jnp.exp(v_out - jnp.max(v_out, axis=-1, keepdims=True))
    e_ref[...] = e_out.astype(jnp.int32)
    gate_ref[...] = p / jnp.sum(p, axis=-1, keepdims=True)

    sel = jnp.zeros(logits.shape, F32)
    for k in range(TOP_K):
        sel = jnp.where(picks[k], 1.0, sel)
    earlier = (lax.broadcasted_iota(jnp.int32, (tm, tm), 1)
               < lax.broadcasted_iota(jnp.int32, (tm, tm), 0))
    before = jnp.dot(jnp.where(earlier, 1.0, 0.0).astype(BF16), sel.astype(BF16),
                     preferred_element_type=F32) + run_s[...]
    rank = jnp.zeros(e_ref.shape, F32)
    for k in range(TOP_K):
        rank_k = jnp.sum(jnp.where(picks[k], before, 0.0), axis=-1, keepdims=True)
        rank = jnp.where(slot == k, rank_k, rank)
    rank_ref[...] = rank.astype(jnp.int32)
    run_s[...] = run_s[...] + jnp.sum(sel, axis=0, keepdims=True)
    count_ref[...] = run_s[...]


def _mix_ln_router(a, w, res, g, b, w_router, b_router, *, alpha, tm=256):
    m, k = a.shape
    d = w.shape[1]
    n_exp = w_router.shape[1]
    assert m % tm == 0
    w_hi = w_router.astype(BF16)
    w_lo = (w_router - w_hi.astype(F32)).astype(BF16)
    full = lambda shape: pl.BlockSpec(shape, lambda i: (0,) * len(shape))
    return pl.pallas_call(
        functools.partial(_mix_ln_router_kernel, alpha=alpha),
        grid=(m // tm,),
        in_specs=[pl.BlockSpec((tm, k), lambda i: (i, 0)), full((k, d)),
                  pl.BlockSpec((tm, d), lambda i: (i, 0)), full((1, d)), full((1, d)),
                  full((d, 2 * n_exp)), full((1, n_exp))],
        out_specs=[pl.BlockSpec((tm, d), lambda i: (i, 0)),
                   pl.BlockSpec((tm, TOP_K), lambda i: (i, 0)),
                   pl.BlockSpec((tm, TOP_K), lambda i: (i, 0)),
                   pl.BlockSpec((tm, TOP_K), lambda i: (i, 0)),
                   full((1, n_exp))],
        out_shape=[jax.ShapeDtypeStruct((m, d), F32),
                   jax.ShapeDtypeStruct((m, TOP_K), jnp.int32),
                   jax.ShapeDtypeStruct((m, TOP_K), F32),
                   jax.ShapeDtypeStruct((m, TOP_K), jnp.int32),
                   jax.ShapeDtypeStruct((1, n_exp), F32)],
        scratch_shapes=[pltpu.VMEM((1, n_exp), F32)],
        compiler_params=_params(("arbitrary",)),
        name="mix_ln_router",
    )(a, w, res, g.reshape(1, d), b.reshape(1, d), jnp.concatenate([w_hi, w_lo], axis=1),
      b_router.reshape(1, n_exp))


def _dispatch_kernel(dest_ref, empty_lo_ref, empty_hi_ref, x_ref, xs_ref, zero_s, sems, zsem):
    tt = x_ref.shape[0]
    i = pl.program_id(0)

    def issue(g, carry):
        row0 = pl.multiple_of(g * SUBLANES, SUBLANES)
        slot0 = (i * tt + row0) * TOP_K
        for j in range(SUBLANES):
            for k in range(TOP_K):
                pltpu.make_async_copy(
                    x_ref.at[pl.ds(row0 + j, 1), :],
                    xs_ref.at[pl.ds(dest_ref[slot0 + (j * TOP_K + k)], 1), :],
                    sems.at[k]).start(priority=k % 2)
        return carry

    lax.fori_loop(0, tt // SUBLANES, issue, 0)

    for k in range(TOP_K):
        pltpu.make_async_copy(x_ref, xs_ref.at[pl.ds(0, tt), :], sems.at[k]).wait()

    @pl.when(i == pl.num_programs(0) - 1)
    def _():
        zero_s[...] = jnp.zeros(zero_s.shape, zero_s.dtype)

        def zero_copy(row):
            return pltpu.make_async_copy(zero_s, xs_ref.at[pl.ds(row, 1), :], zsem)

        def fill(e, carry):
            def one(row, c):
                zero_copy(row).start()
                return c
            return lax.fori_loop(empty_lo_ref[e], empty_hi_ref[e], one, carry)

        def drain(e, carry):
            def one(row, c):
                zero_copy(row).wait()
                return c
            return lax.fori_loop(empty_lo_ref[e], empty_hi_ref[e], one, carry)

        lax.fori_loop(0, empty_lo_ref.shape[0], fill, 0)
        lax.fori_loop(0, empty_lo_ref.shape[0], drain, 0)


def _dispatch(x, dest_flat, empty_lo, empty_hi, n_rows, *, tt=256):
    t, d = x.shape
    assert t % tt == 0
    grid_spec = pltpu.PrefetchScalarGridSpec(
        num_scalar_prefetch=3,
        grid=(t // tt,),
        in_specs=[pl.BlockSpec((tt, d), lambda i, *_: (i, 0))],
        out_specs=pl.BlockSpec(memory_space=pl.ANY),
        scratch_shapes=[pltpu.VMEM((1, d), F32),
                        pltpu.SemaphoreType.DMA((TOP_K,)),
                        pltpu.SemaphoreType.DMA],
    )
    return pl.pallas_call(
        _dispatch_kernel,
        grid_spec=grid_spec,
        out_shape=jax.ShapeDtypeStruct((n_rows, d), F32),
        compiler_params=_params(("arbitrary",)),
        name="moe_dispatch",
    )(dest_flat, empty_lo, empty_hi, x)


def _expert_kernel(block_e_ref, n_valid_ref, xs_ref, wgu_ref, bgu_ref, wd_ref, bd_ref, y_ref,
                   wgu_b, wd_b):
    f = wd_ref.shape[0]
    i = pl.program_id(0)
    live = i < n_valid_ref[0]

    @pl.when((i == 0) | (block_e_ref[i] != block_e_ref[jnp.maximum(i - 1, 0)]))
    def _():
        wgu_b[...] = wgu_ref[...].astype(BF16)
        wd_b[...] = wd_ref[...].astype(BF16)

    @pl.when(jnp.logical_not(live))
    def _():
        y_ref[...] = jnp.zeros(y_ref.shape, y_ref.dtype)

    @pl.when(live)
    def _():
        gu = jnp.dot(xs_ref[...].astype(BF16), wgu_b[...],
                     preferred_element_type=F32) + bgu_ref[...]
        glu = jnp.minimum(gu[:, :f], SWIGLU_LIMIT)
        lin = jnp.clip(gu[:, f:], -SWIGLU_LIMIT, SWIGLU_LIMIT)
        h = glu * jax.nn.sigmoid(SWIGLU_ALPHA * glu) * (lin + 1.0)
        y_ref[...] = jnp.dot(h.astype(BF16), wd_b[...],
                             preferred_element_type=F32) + bd_ref[...]


def _experts(xs, block_e, n_valid, w_gu, b_gu, w_down, b_down, *, tm):
    r, d = xs.shape
    n_exp, _, f2 = w_gu.shape
    f = f2 // 2
    nb = r // tm
    live = lambda i, be, nv: jnp.minimum(i, nv[0] - 1)
    grid_spec = pltpu.PrefetchScalarGridSpec(
        num_scalar_prefetch=2,
        grid=(nb,),
        in_specs=[pl.BlockSpec((tm, d), lambda i, be, nv: (live(i, be, nv), 0)),
                  pl.BlockSpec((None, d, f2), lambda i, be, nv: (be[i], 0, 0),
                               pipeline_mode=pl.Buffered(1)),
                  pl.BlockSpec((None, 1, f2), lambda i, be, nv: (be[i], 0, 0)),
                  pl.BlockSpec((None, f, d), lambda i, be, nv: (be[i], 0, 0),
                               pipeline_mode=pl.Buffered(1)),
                  pl.BlockSpec((None, 1, d), lambda i, be, nv: (be[i], 0, 0))],
        out_specs=pl.BlockSpec((tm, d), lambda i, be, nv: (i, 0)),
        scratch_shapes=[pltpu.VMEM((d, f2), BF16), pltpu.VMEM((f, d), BF16)],
    )
    return pl.pallas_call(
        _expert_kernel,
        grid_spec=grid_spec,
        out_shape=jax.ShapeDtypeStruct((r, d), F32),
        compiler_params=_params(("arbitrary",)),
        name="moe_experts",
    )(block_e, n_valid, xs, w_gu, b_gu.reshape(n_exp, 1, f2), w_down,
      b_down.reshape(n_exp, 1, d))


def _combine_ln_kernel(dest_ref, y_ref, gate_ref, res_ref, g_ref, b_ref, o_ref, buf, sems,
                       *, alpha):
    tt = res_ref.shape[0]
    i = pl.program_id(0)
    n = pl.num_programs(0)
    slot = i % 2

    def issue(step, s):
        def body(g, carry):
            row0 = pl.multiple_of(g * SUBLANES, SUBLANES)
            slot0 = (step * tt + row0) * TOP_K
            for j in range(SUBLANES):
                for k in range(TOP_K):
                    pltpu.make_async_copy(
                        y_ref.at[pl.ds(dest_ref[slot0 + (j * TOP_K + k)], 1), :],
                        buf.at[s, k, pl.ds(row0 + j, 1), :],
                        sems.at[s, k]).start(priority=k % 2)
            return carry
        lax.fori_loop(0, tt // SUBLANES, body, 0)

    @pl.when(i == 0)
    def _():
        issue(0, 0)

    for s in range(2):
        @pl.when((i + 1 < n) & (slot != s))
        def _(s=s):
            issue(i + 1, s)

    for k in range(TOP_K):
        pltpu.make_async_copy(y_ref.at[pl.ds(0, tt), :], buf.at[slot, k], sems.at[slot, k]).wait()

    gate = gate_ref[...]
    ffn = buf[slot, 0] * gate[:, 0:1]
    for k in range(1, TOP_K):
        ffn = ffn + buf[slot, k] * gate[:, k:k + 1]
    o_ref[...] = _layer_norm(alpha * res_ref[...] + ffn, g_ref[...], b_ref[...])


def _combine_ln(y_rows, dest_flat, gates, res, g, b, *, alpha, tt=128):
    t, d = res.shape
    assert t % tt == 0
    grid_spec = pltpu.PrefetchScalarGridSpec(
        num_scalar_prefetch=1,
        grid=(t // tt,),
        in_specs=[pl.BlockSpec(memory_space=pl.ANY),
                  pl.BlockSpec((tt, TOP_K), lambda i, dest: (i, 0)),
                  pl.BlockSpec((tt, d), lambda i, dest: (i, 0)),
                  pl.BlockSpec((1, d), lambda i, dest: (0, 0)),
                  pl.BlockSpec((1, d), lambda i, dest: (0, 0))],
        out_specs=pl.BlockSpec((tt, d), lambda i, dest: (i, 0)),
        scratch_shapes=[pltpu.VMEM((2, TOP_K, tt, d), F32),
                        pltpu.SemaphoreType.DMA((2, TOP_K))],
    )
    return pl.pallas_call(
        functools.partial(_combine_ln_kernel, alpha=alpha),
        grid_spec=grid_spec,
        out_shape=jax.ShapeDtypeStruct((t, d), F32),
        compiler_params=_params(("arbitrary",)),
        name="moe_combine_ln",
    )(dest_flat, y_rows, gates, res, g.reshape(1, d), b.reshape(1, d))


def _route(top_e, rank, counts, tm, n_blocks):
    n_exp = counts.shape[-1]
    counts = counts.reshape(n_exp).astype(jnp.int32)
    padded = (counts + tm - 1) // tm * tm
    pad_end = jnp.cumsum(padded)
    pad_start = pad_end - padded
    hit = top_e[:, :, None] == jnp.arange(n_exp, dtype=jnp.int32)
    dest = rank + jnp.sum(jnp.where(hit, pad_start, 0), axis=-1, dtype=jnp.int32)
    first_row = jnp.arange(n_blocks, dtype=jnp.int32) * tm
    block_e = jnp.minimum(jnp.sum(pad_end[None, :] <= first_row[:, None], axis=1, dtype=jnp.int32),
                          n_exp - 1)
    n_valid = (pad_end[-1] // tm).astype(jnp.int32).reshape(1)
    empty_lo = jnp.concatenate([pad_start + counts, pad_end[-1:]])
    empty_hi = jnp.concatenate([pad_end, jnp.full((1,), n_blocks * tm, jnp.int32)])
    return dest.reshape(-1), empty_lo, empty_hi, block_e, n_valid


def _moe_ln(x, top_e, gates, rank, counts, w_gu, b_gu, w_down, b_down, g, b, *, alpha, tm=256):
    t, _ = x.shape
    n_exp = w_gu.shape[0]
    n_blocks = t * TOP_K // tm + n_exp
    dest, pad_lo, pad_hi, block_e, n_valid = _route(top_e, rank, counts, tm, n_blocks)
    xs = _dispatch(x, dest, pad_lo, pad_hi, n_blocks * tm)
    y_rows = _experts(xs, block_e, n_valid, w_gu, b_gu, w_down, b_down, tm=tm)
    return _combine_ln(y_rows, dest, gates, x, g, b, alpha=alpha)


def _attn_kernel(*refs, scale):
    ng = len(DIL_GROUPS)
    q_refs = refs[0:ng]
    kc_refs = refs[ng:2 * ng]
    kp_refs = refs[2 * ng:3 * ng]
    vc_refs = refs[3 * ng:4 * ng]
    vp_refs = refs[4 * ng:5 * ng]
    o_ref = refs[5 * ng]
    og_s, lse_s = refs[5 * ng + 1:]
    blk = ATTN_BLK
    nblk = ATTN_SUPER // blk
    qi = lax.broadcasted_iota(jnp.int32, (nblk, blk, blk), 1)
    kj = lax.broadcasted_iota(jnp.int32, (nblk, blk, blk), 2)
    cur_ok = kj <= qi
    prev_ok = kj >= qi
    bidx = lax.broadcasted_iota(jnp.int32, (nblk, blk, blk), 0)
    seq_start = jnp.full((nblk, blk, blk), pl.program_id(1), jnp.int32) == 0
    qk = (((2,), (2,)), ((0,), (0,)))
    pv = (((2,), (1,)), ((0,), (0,)))

    for g, (_, dil) in enumerate(DIL_GROUPS):
        n_sub = nblk // dil
        span = n_sub * blk

        def blocks(ref, r, start_blk=0, count=n_sub, dil=dil):
            if dil == 1:
                rows = ref[start_blk * blk:(start_blk + count) * blk, :]
            else:
                rows = ref[pl.ds(r + dil * blk * start_blk, count * blk, stride=dil), :]
            return rows.reshape(count, blk, HEAD_DIM).astype(BF16)

        def with_prev(cur_ref, prev_ref, r, n_sub=n_sub):
            cur = blocks(cur_ref, r)
            last_of_prev = blocks(prev_ref, r, n_sub - 1, 1)
            if n_sub == 1:
                return cur, last_of_prev
            return cur, jnp.concatenate([last_of_prev, cur[:n_sub - 1]], axis=0)

        q, k_cur, k_prev, v_cur, v_prev = [], [], [], [], []
        for r in range(dil):
            q.append(blocks(q_refs[g], r))
            kc, kp = with_prev(kc_refs[g], kp_refs[g], r)
            vc, vp = with_prev(vc_refs[g], vp_refs[g], r)
            k_cur.append(kc)
            k_prev.append(kp)
            v_cur.append(vc)
            v_prev.append(vp)
        cat = lambda parts: parts[0] if len(parts) == 1 else jnp.concatenate(parts, axis=0)
        q, k_cur, k_prev, v_cur, v_prev = map(cat, (q, k_cur, k_prev, v_cur, v_prev))

        no_prev = seq_start & (bidx % n_sub == 0)
        s_cur = lax.dot_general(q, k_cur, qk, preferred_element_type=F32) * scale
        s_prev = lax.dot_general(q, k_prev, qk, preferred_element_type=F32) * scale
        s_cur = jnp.where(cur_ok, s_cur, -jnp.inf)
        s_prev = jnp.where(prev_ok & jnp.logical_not(no_prev), s_prev, -jnp.inf)
        m = jnp.maximum(jnp.max(s_cur, axis=-1, keepdims=True),
                        jnp.max(s_prev, axis=-1, keepdims=True))
        p_cur = jnp.exp(s_cur - m)
        p_prev = jnp.exp(s_prev - m)
        den = (jnp.sum(p_cur, axis=-1, keepdims=True)
               + jnp.sum(p_prev, axis=-1, keepdims=True))
        o = (lax.dot_general(p_cur.astype(BF16), v_cur, pv, preferred_element_type=F32)
             + lax.dot_general(p_prev.astype(BF16), v_prev, pv, preferred_element_type=F32)) / den
        lse = jnp.broadcast_to(m + jnp.log(den), (nblk, blk, LANES))
        for r in range(dil):
            o_r = o[r * n_sub:(r + 1) * n_sub].reshape(span, HEAD_DIM)
            lse_r = lse[r * n_sub:(r + 1) * n_sub].reshape(span, LANES)
            if dil == 1:
                og_s[g] = o_r
                lse_s[g] = lse_r
            else:
                og_s[g, pl.ds(r, span, stride=dil), :] = o_r
                lse_s[g, pl.ds(r, span, stride=dil), :] = lse_r

    top = lse_s[0]
    for g in range(1, ng):
        top = jnp.maximum(top, lse_s[g])
    num = jnp.zeros(top.shape, F32)
    den = jnp.zeros(top.shape, F32)
    for g in range(ng):
        w = jnp.exp(lse_s[g] - top)
        num = num + w * og_s[g]
        den = den + w
    o_ref[...] = (num / den).astype(o_ref.dtype)


def _attention(q, k, v, *, batch, hpg):
    t = q.shape[0]
    seq = t // batch
    assert seq % ATTN_SUPER == 0
    ns = seq // ATTN_SUPER
    ng = len(DIL_GROUPS)

    def cur(off):
        return lambda b, m, h, off=off: (b * ns + m, off + h)

    def prev(off):
        return lambda b, m, h, off=off: (b * ns + jnp.maximum(m - 1, 0), off + h)

    blk = lambda index_map: pl.BlockSpec((ATTN_SUPER, HEAD_DIM), index_map)
    in_specs = ([blk(cur(g * hpg)) for g in range(ng)]
                + [blk(cur(g * hpg)) for g in range(ng)]
                + [blk(prev(g * hpg)) for g in range(ng)]
                + [blk(cur(g * hpg)) for g in range(ng)]
                + [blk(prev(g * hpg)) for g in range(ng)])
    return pl.pallas_call(
        functools.partial(_attn_kernel, scale=HEAD_DIM ** -0.5),
        grid=(batch, ns, hpg),
        in_specs=in_specs,
        out_specs=pl.BlockSpec((ATTN_SUPER, HEAD_DIM), lambda b, m, h: (b * ns + m, h)),
        out_shape=jax.ShapeDtypeStruct((t, hpg * HEAD_DIM), BF16),
        scratch_shapes=[pltpu.VMEM((ng, ATTN_SUPER, HEAD_DIM), F32),
                        pltpu.VMEM((ng, ATTN_SUPER, LANES), F32)],
        compiler_params=_params(("parallel", "arbitrary", "arbitrary")),
        name="dilated_attn",
    )(*([q] * ng + [k] * (2 * ng) + [v] * (2 * ng)))


def _rope_tables(positions):
    half = ROT_DIM // 2
    inv = jnp.power(ROPE_THETA, -jnp.arange(half, dtype=F32) * (2.0 / ROT_DIM))
    ang = positions.reshape(-1).astype(F32)[:, None] * inv
    cos, sin = jnp.cos(ang), jnp.sin(ang)
    rest = HEAD_DIM - ROT_DIM
    cos_t = jnp.concatenate([cos, cos, jnp.ones((ang.shape[0], rest), F32)], axis=1)
    sin_t = jnp.concatenate([-sin, sin, jnp.zeros((ang.shape[0], rest), F32)], axis=1)
    return cos_t, sin_t


def kernel(x, positions, a_w_in, a_conv_w, a_conv_b, a_w_rg, a_b_rg, a_w_ig, a_b_ig, a_lambda,
           a_w_out, kv_w, b_w_q, b_w_o, mix_ln_g, mix_ln_b, moe_w_router, moe_b_router,
           moe_w_gu, moe_b_gu, moe_w_down, moe_b_down, ffn_ln_g, ffn_ln_b):
    batch, seq, d = x.shape
    depth = mix_ln_g.shape[0]
    n_a = a_w_in.shape[0]
    c = a_w_in.shape[2] // 2
    attn_w = b_w_q.shape[2]
    hpg = b_w_o.shape[1] // HEAD_DIM
    alpha = (2 * depth) ** 0.25
    t = batch * seq

    x = x.reshape(t, d)
    cos_t, sin_t = _rope_tables(positions)
    k_sh = v_sh = None
    for layer in range(depth):
        if layer < n_a:
            gate = _proj(x, a_w_in[layer][:, :c].astype(BF16), epilogue="gelu")
            u = _proj(x, a_w_in[layer][:, c:].astype(BF16))
            mixed = _rglru(gate, u, a_conv_w[layer], a_conv_b[layer], a_w_rg[layer], a_b_rg[layer],
                           a_w_ig[layer], a_b_ig[layer], a_lambda[layer], batch=batch)
            w_out = a_w_out[layer]
        else:
            j = layer - n_a
            q = _proj(x, b_w_q[j].astype(BF16), epilogue="rope", cos_t=cos_t, sin_t=sin_t)
            mixed = _attention(q, k_sh, v_sh, batch=batch, hpg=hpg)
            w_out = b_w_o[j]
        x, top_e, gates, rank, counts = _mix_ln_router(
            mixed, w_out.astype(BF16), x, mix_ln_g[layer], mix_ln_b[layer], moe_w_router[layer],
            moe_b_router[layer], alpha=alpha)
        x = _moe_ln(x, top_e, gates, rank, counts, moe_w_gu[layer], moe_b_gu[layer],
                    moe_w_down[layer], moe_b_down[layer], ffn_ln_g[layer],
                    ffn_ln_b[layer], alpha=alpha)
        if layer == n_a - 1:
            k_sh = _proj(x, kv_w[:, :attn_w].astype(BF16), epilogue="rope",
                         cos_t=cos_t, sin_t=sin_t)
            v_sh = _proj(x, kv_w[:, attn_w:].astype(BF16))
    return x.reshape(batch, seq, d)
```

```python
import functools

import jax
import jax.numpy as jnp
from jax import lax
from jax.experimental import pallas as pl
from jax.experimental.pallas import tpu as pltpu

F32 = jnp.float32
BF16 = jnp.bfloat16

CONV_W = 4
LRU_C = 8.0
HEAD_DIM = 128
ROT_DIM = HEAD_DIM // 4
ROPE_THETA = 500000.0
DIL_GROUPS = ((128, 1), (512, 4), (2048, 16))
ATTN_BLK = 128
ATTN_SUPER = 2048
TOP_K = 4
SWIGLU_LIMIT = 7.0
SWIGLU_ALPHA = 1.702
LN_EPS = 1e-5

LANES = 128
SUBLANES = 8
VMEM_LIMIT = 56 * 1024 * 1024


def _params(sem):
    return pltpu.CompilerParams(dimension_semantics=sem, vmem_limit_bytes=VMEM_LIMIT)


def _pick(n, candidates):
    for c in candidates:
        if n % c == 0:
            return c
    raise ValueError(f"no tile in {candidates} divides {n}")


def _layer_norm(y, g, b):
    mu = jnp.mean(y, axis=-1, keepdims=True)
    yc = y - mu
    var = jnp.mean(yc * yc, axis=-1, keepdims=True)
    return yc * lax.rsqrt(var + LN_EPS) * g + b


def _rope_chunk(c, cos_t, sin_t):
    lane = lax.broadcasted_iota(jnp.int32, c.shape, 1)
    partner = jnp.where(lane < ROT_DIM // 2,
                        pltpu.roll(c, LANES - ROT_DIM // 2, 1),
                        pltpu.roll(c, ROT_DIM // 2, 1))
    return c * cos_t + partner * sin_t


def _proj_kernel(x_ref, w_ref, *rest, epilogue):
    if epilogue == "rope":
        cos_ref, sin_ref, o_ref = rest
    else:
        (o_ref,) = rest
    acc = jnp.dot(x_ref[...].astype(BF16), w_ref[...], preferred_element_type=F32)
    if epilogue == "gelu":
        o_ref[...] = 0.5 * acc * (1.0 + jnp.tanh(
            0.7978845608028654 * (acc + 0.044715 * (acc * acc * acc))))
    elif epilogue == "rope":
        cos_t = cos_ref[...]
        sin_t = sin_ref[...]
        for h in range(acc.shape[1] // LANES):
            sl = slice(h * LANES, (h + 1) * LANES)
            o_ref[:, sl] = _rope_chunk(acc[:, sl], cos_t, sin_t)
    else:
        o_ref[...] = acc


def _proj(x, w, *, epilogue="none", cos_t=None, sin_t=None, tm=256):
    m, k = x.shape
    n = w.shape[1]
    assert m % tm == 0 and n % LANES == 0
    in_specs = [pl.BlockSpec((tm, k), lambda i: (i, 0)),
                pl.BlockSpec((k, n), lambda i: (0, 0))]
    args = [x, w]
    if epilogue == "rope":
        in_specs += [pl.BlockSpec((tm, LANES), lambda i: (i, 0))] * 2
        args += [cos_t, sin_t]
    return pl.pallas_call(
        functools.partial(_proj_kernel, epilogue=epilogue),
        grid=(m // tm,),
        in_specs=in_specs,
        out_specs=pl.BlockSpec((tm, n), lambda i: (i, 0)),
        out_shape=jax.ShapeDtypeStruct((m, n), F32),
        compiler_params=_params(("parallel",)),
        name=f"proj_{epilogue}",
    )(*args)


def _band_plan(c, bw, ct):
    plan = []
    for j in range(c // ct):
        c0 = j * ct
        lo = (c0 // bw) * bw
        hi = ((c0 + ct - 1) // bw + 1) * bw
        ks = (lo // LANES) * LANES
        kb = -(-(hi - ks) // LANES) * LANES
        ks = min(ks, c - kb)
        plan.append((ks, kb))
    return plan


def _band_weights(w_blocks, plan, ct):
    nb, bw, _ = w_blocks.shape
    c = nb * bw
    dense = jnp.zeros((nb, bw, nb, bw), w_blocks.dtype)
    dense = dense.at[jnp.arange(nb), :, jnp.arange(nb), :].set(w_blocks)
    dense = dense.reshape(c, c)
    kb_max = max(kb for _, kb in plan)
    slabs = []
    for j, (ks, kb) in enumerate(plan):
        slab = dense[ks:ks + kb, j * ct:(j + 1) * ct]
        slabs.append(jnp.pad(slab, ((0, kb_max - kb), (0, 0))))
    return jnp.stack(slabs).astype(BF16)


def _sigmoid(x):
    return 0.5 * jnp.tanh(0.5 * x) + 0.5


def _scan8(a, b):
    row = lax.broadcasted_iota(jnp.int32, a.shape, 0)
    for d in (1, 2, 4):
        a_prev = pltpu.roll(a, d, 0)
        b_prev = pltpu.roll(b, d, 0)
        keep = row >= d
        b = jnp.where(keep, a * b_prev + b, b)
        a = jnp.where(keep, a * a_prev, a)
    return a, b


def _rglru_kernel(gate_ref, u_ref, cw_ref, cb_ref, wrg_ref, brg_ref, wig_ref, big_ref,
                  lam_ref, o_ref, ubuf, uc_s, hcarry, *, plan, ct):
    ts = u_ref.shape[0]
    i = pl.program_id(1)

    @pl.when(i == 0)
    def _():
        ubuf[0:SUBLANES, :] = jnp.zeros((SUBLANES, ubuf.shape[1]), F32)
        hcarry[...] = jnp.zeros(hcarry.shape, F32)

    ubuf[SUBLANES:SUBLANES + ts, :] = u_ref[...]
    base = SUBLANES - (CONV_W - 1)
    conv = ubuf[base:base + ts, :] * cw_ref[0:1, :]
    for k in range(1, CONV_W):
        conv = conv + ubuf[base + k:base + k + ts, :] * cw_ref[k:k + 1, :]
    uc_s[...] = cb_ref[...] + conv
    ubuf[0:SUBLANES, :] = ubuf[ts:ts + SUBLANES, :]

    lam = lam_ref[...]
    softplus_neg_lam = jnp.maximum(-lam, 0.0) + jnp.log1p(jnp.exp(-jnp.abs(lam)))

    for j, (ks, kb) in enumerate(plan):
        cs = slice(j * ct, (j + 1) * ct)
        band = uc_s[:, ks:ks + kb].astype(BF16)
        r_pre = jnp.dot(band, wrg_ref[j, 0:kb, :], preferred_element_type=F32) + brg_ref[:, cs]
        i_pre = jnp.dot(band, wig_ref[j, 0:kb, :], preferred_element_type=F32) + big_ref[:, cs]
        carry = hcarry[:, cs]
        for c0 in range(0, ts, SCAN_ROWS):
            rs = slice(c0, c0 + SCAN_ROWS)
            log_a = -LRU_C * _sigmoid(r_pre[rs]) * softplus_neg_lam[:, cs]
            a = jnp.exp(log_a)
            b = (jnp.sqrt(-jnp.tanh(log_a) * (a * a + 1.0)) * _sigmoid(i_pre[rs])
                 * uc_s[rs, cs])
            outs = []
            for s0 in range(0, SCAN_ROWS, SUBLANES):
                a8, b8 = _scan8(a[s0:s0 + SUBLANES], b[s0:s0 + SUBLANES])
                h8 = b8 + a8 * carry
                carry = jnp.broadcast_to(h8[SUBLANES - 1:SUBLANES, :], h8.shape)
                outs.append(h8 * gate_ref[c0 + s0:c0 + s0 + SUBLANES, cs])
            o_ref[rs, cs] = jnp.concatenate(outs, axis=0).astype(o_ref.dtype)
        hcarry[:, cs] = carry


SCAN_ROWS = 64


def _rglru(gate, u, conv_w, conv_b, w_rg, b_rg, w_ig, b_ig, lam, *, batch, ts=256):
    t, c = u.shape
    nb, bw, _ = w_rg.shape
    ct = _pick(c, (384, 256, 128))
    plan = _band_plan(c, bw, ct)
    wrg = _band_weights(w_rg, plan, ct)
    wig = _band_weights(w_ig, plan, ct)
    seq = t // batch
    assert seq % ts == 0
    nt = seq // ts
    row = lambda v: v.reshape(1, c)
    vec = pl.BlockSpec((1, c), lambda b, i: (0, 0))
    slab = pl.BlockSpec(wrg.shape, lambda b, i: (0, 0, 0))
    return pl.pallas_call(
        functools.partial(_rglru_kernel, plan=plan, ct=ct),
        grid=(batch, nt),
        in_specs=[pl.BlockSpec((ts, c), lambda b, i: (b * nt + i, 0)),
                  pl.BlockSpec((ts, c), lambda b, i: (b * nt + i, 0)),
                  pl.BlockSpec((CONV_W, c), lambda b, i: (0, 0)),
                  vec, slab, vec, slab, vec, vec],
        out_specs=pl.BlockSpec((ts, c), lambda b, i: (b * nt + i, 0)),
        out_shape=jax.ShapeDtypeStruct((t, c), BF16),
        scratch_shapes=[pltpu.VMEM((ts + SUBLANES, c), F32),
                        pltpu.VMEM((ts, c), F32),
                        pltpu.VMEM((SUBLANES, c), F32)],
        compiler_params=_params(("arbitrary", "arbitrary")),
        name="rglru",
    )(gate, u, conv_w, row(conv_b), wrg, row(b_rg), wig, row(b_ig), row(lam))


def _mix_ln_router_kernel(a_ref, w_ref, res_ref, g_ref, b_ref, wr2_ref, br_ref,
                          xn_ref, e_ref, gate_ref, rank_ref, count_ref, run_s, *, alpha):
    tm = a_ref.shape[0]
    n_exp = br_ref.shape[1]

    @pl.when(pl.program_id(0) == 0)
    def _():
        run_s[...] = jnp.zeros(run_s.shape, F32)

    mix = jnp.dot(a_ref[...], w_ref[...], preferred_element_type=F32)
    xn = _layer_norm(alpha * res_ref[...] + mix, g_ref[...], b_ref[...])
    xn_ref[...] = xn

    x_hi = xn.astype(BF16)
    x_lo = (xn - x_hi.astype(F32)).astype(BF16)
    prod = jnp.dot(jnp.concatenate([x_hi, x_lo], axis=0), wr2_ref[...],
                   preferred_element_type=F32)
    logits = ((prod[:tm, n_exp:] + prod[tm:, :n_exp] + prod[tm:, n_exp:]) + prod[:tm, :n_exp]
              + br_ref[...])

    col = lax.broadcasted_iota(jnp.int32, logits.shape, 1).astype(F32)
    slot = lax.broadcasted_iota(jnp.int32, e_ref.shape, 1)
    e_out = jnp.zeros(e_ref.shape, F32)
    v_out = jnp.zeros(e_ref.shape, F32)
    picks = []
    for k in range(TOP_K):
        top = jnp.max(logits, axis=-1, keepdims=True)
        idx = jnp.min(jnp.where(logits == top, col, float(n_exp)), axis=-1, keepdims=True)
        picks.append(col == idx)
        logits = jnp.where(picks[k], -jnp.inf, logits)
        e_out = jnp.where(slot == k, idx, e_out)
        v_out = jnp.where(slot == k, top, v_out)
    p = jnp.exp(v_out - jnp.max(v_out, axis=-1, keepdims=True))
    e_ref[...] = e_out.astype(jnp.int32)
    gate_ref[...] = p / jnp.sum(p, axis=-1, keepdims=True)

    sel = jnp.zeros(logits.shape, F32)
    for k in range(TOP_K):
        sel = jnp.where(picks[k], 1.0, sel)
    earlier = (lax.broadcasted_iota(jnp.int32, (tm, tm), 1)
               < lax.broadcasted_iota(jnp.int32, (tm, tm), 0))
    before = jnp.dot(jnp.where(earlier, 1.0, 0.0).astype(BF16), sel.astype(BF16),
                     preferred_element_type=F32) + run_s[...]
    rank = jnp.zeros(e_ref.shape, F32)
    for k in range(TOP_K):
        rank_k = jnp.sum(jnp.where(picks[k], before, 0.0), axis=-1, keepdims=True)
        rank = jnp.where(slot == k, rank_k, rank)
    rank_ref[...] = rank.astype(jnp.int32)
    run_s[...] = run_s[...] + jnp.sum(sel, axis=0, keepdims=True)
    count_ref[...] = run_s[...]


def _mix_ln_router(a, w, res, g, b, w_router, b_router, *, alpha, tm=256):
    m, k = a.shape
    d = w.shape[1]
    n_exp = w_router.shape[1]
    assert m % tm == 0
    w_hi = w_router.astype(BF16)
    w_lo = (w_router - w_hi.astype(F32)).astype(BF16)
    full = lambda shape: pl.BlockSpec(shape, lambda i: (0,) * len(shape))
    return pl.pallas_call(
        functools.partial(_mix_ln_router_kernel, alpha=alpha),
        grid=(m // tm,),
        in_specs=[pl.BlockSpec((tm, k), lambda i: (i, 0)), full((k, d)),
                  pl.BlockSpec((tm, d), lambda i: (i, 0)), full((1, d)), full((1, d)),
                  full((d, 2 * n_exp)), full((1, n_exp))],
        out_specs=[pl.BlockSpec((tm, d), lambda i: (i, 0)),
                   pl.BlockSpec((tm, TOP_K), lambda i: (i, 0)),
                   pl.BlockSpec((tm, TOP_K), lambda i: (i, 0)),
                   pl.BlockSpec((tm, TOP_K), lambda i: (i, 0)),
                   full((1, n_exp))],
        out_shape=[jax.ShapeDtypeStruct((m, d), F32),
                   jax.ShapeDtypeStruct((m, TOP_K), jnp.int32),
                   jax.ShapeDtypeStruct((m, TOP_K), F32),
                   jax.ShapeDtypeStruct((m, TOP_K), jnp.int32),
                   jax.ShapeDtypeStruct((1, n_exp), F32)],
        scratch_shapes=[pltpu.VMEM((1, n_exp), F32)],
        compiler_params=_params(("arbitrary",)),
        name="mix_ln_router",
    )(a, w, res, g.reshape(1, d), b.reshape(1, d), jnp.concatenate([w_hi, w_lo], axis=1),
      b_router.reshape(1, n_exp))


def _dispatch_kernel(dest_ref, empty_lo_ref, empty_hi_ref, x_ref, xs_ref, zero_s, sems, zsem):
    tt = x_ref.shape[0]
    i = pl.program_id(0)

    def issue(g, carry):
        row0 = pl.multiple_of(g * SUBLANES, SUBLANES)
        slot0 = (i * tt + row0) * TOP_K
        for j in range(SUBLANES):
            for k in range(TOP_K):
                pltpu.make_async_copy(
                    x_ref.at[pl.ds(row0 + j, 1), :],
                    xs_ref.at[pl.ds(dest_ref[slot0 + (j * TOP_K + k)], 1), :],
                    sems.at[k]).start(priority=k % 2)
        return carry

    lax.fori_loop(0, tt // SUBLANES, issue, 0)

    for k in range(TOP_K):
        pltpu.make_async_copy(x_ref, xs_ref.at[pl.ds(0, tt), :], sems.at[k]).wait()

    @pl.when(i == pl.num_programs(0) - 1)
    def _():
        zero_s[...] = jnp.zeros(zero_s.shape, zero_s.dtype)

        def zero_copy(row):
            return pltpu.make_async_copy(zero_s, xs_ref.at[pl.ds(row, 1), :], zsem)

        def fill(e, carry):
            def one(row, c):
                zero_copy(row).start()
                return c
            return lax.fori_loop(empty_lo_ref[e], empty_hi_ref[e], one, carry)

        def drain(e, carry):
            def one(row, c):
                zero_copy(row).wait()
                return c
            return lax.fori_loop(empty_lo_ref[e], empty_hi_ref[e], one, carry)

        lax.fori_loop(0, empty_lo_ref.shape[0], fill, 0)
        lax.fori_loop(0, empty_lo_ref.shape[0], drain, 0)


def _dispatch(x, dest_flat, empty_lo, empty_hi, n_rows, *, tt=256):
    t, d = x.shape
    assert t % tt == 0
    grid_spec = pltpu.PrefetchScalarGridSpec(
        num_scalar_prefetch=3,
        grid=(t // tt,),
        in_specs=[pl.BlockSpec((tt, d), lambda i, *_: (i, 0))],
        out_specs=pl.BlockSpec(memory_space=pl.ANY),
        scratch_shapes=[pltpu.VMEM((1, d), F32),
                        pltpu.SemaphoreType.DMA((TOP_K,)),
                        pltpu.SemaphoreType.DMA],
    )
    return pl.pallas_call(
        _dispatch_kernel,
        grid_spec=grid_spec,
        out_shape=jax.ShapeDtypeStruct((n_rows, d), F32),
        compiler_params=_params(("arbitrary",)),
        name="moe_dispatch",
    )(dest_flat, empty_lo, empty_hi, x)


def _expert_kernel(block_e_ref, n_valid_ref, next_e_ref, xs_ref, wgu_hbm, bgu_ref, wd_hbm, bd_ref,
                   y_ref, wgu_f, wd_f, wgu_b, wd_b, sems, *, layer):
    f = wd_b.shape[0]
    i = pl.program_id(0)
    live = i < n_valid_ref[0]
    expert = block_e_ref[i]

    def fetch(e):
        return (pltpu.make_async_copy(wgu_hbm.at[layer, e], wgu_f, sems.at[0]),
                pltpu.make_async_copy(wd_hbm.at[layer, e], wd_f, sems.at[1]))

    @pl.when(i == 0)
    def _():
        for copy in fetch(expert):
            copy.start()

    @pl.when(live & ((i == 0) | (expert != block_e_ref[jnp.maximum(i - 1, 0)])))
    def _():
        for copy in fetch(expert):
            copy.wait()
        wgu_b[...] = wgu_f[...].astype(BF16)
        wd_b[...] = wd_f[...].astype(BF16)

        @pl.when(next_e_ref[i] >= 0)
        def _():
            for copy in fetch(next_e_ref[i]):
                copy.start()

    @pl.when(jnp.logical_not(live))
    def _():
        y_ref[...] = jnp.zeros(y_ref.shape, y_ref.dtype)

    @pl.when(live)
    def _():
        gu = jnp.dot(xs_ref[...].astype(BF16), wgu_b[...],
                     preferred_element_type=F32) + bgu_ref[...]
        glu = jnp.minimum(gu[:, :f], SWIGLU_LIMIT)
        lin = jnp.clip(gu[:, f:], -SWIGLU_LIMIT, SWIGLU_LIMIT)
        h = glu * jax.nn.sigmoid(SWIGLU_ALPHA * glu) * (lin + 1.0)
        y_ref[...] = jnp.dot(h.astype(BF16), wd_b[...],
                             preferred_element_type=F32) + bd_ref[...]


def _experts(xs, block_e, n_valid, next_e, w_gu, b_gu, w_down, b_down, *, layer, tm):
    r, d = xs.shape
    depth, n_exp, _, f2 = w_gu.shape
    f = f2 // 2
    nb = r // tm
    live = lambda i, be, nv, ne: jnp.minimum(i, nv[0] - 1)
    grid_spec = pltpu.PrefetchScalarGridSpec(
        num_scalar_prefetch=3,
        grid=(nb,),
        in_specs=[pl.BlockSpec((tm, d), lambda i, be, nv, ne: (live(i, be, nv, ne), 0)),
                  pl.BlockSpec(memory_space=pl.ANY),
                  pl.BlockSpec((None, None, 1, f2), lambda i, be, nv, ne: (layer, be[i], 0, 0)),
                  pl.BlockSpec(memory_space=pl.ANY),
                  pl.BlockSpec((None, None, 1, d), lambda i, be, nv, ne: (layer, be[i], 0, 0))],
        out_specs=pl.BlockSpec((tm, d), lambda i, be, nv, ne: (i, 0)),
        scratch_shapes=[pltpu.VMEM((d, f2), F32), pltpu.VMEM((f, d), F32),
                        pltpu.VMEM((d, f2), BF16), pltpu.VMEM((f, d), BF16),
                        pltpu.SemaphoreType.DMA((2,))],
    )
    return pl.pallas_call(
        functools.partial(_expert_kernel, layer=layer),
        grid_spec=grid_spec,
        out_shape=jax.ShapeDtypeStruct((r, d), F32),
        compiler_params=_params(("arbitrary",)),
        name="moe_experts",
    )(block_e, n_valid, next_e, xs, w_gu, b_gu.reshape(depth, n_exp, 1, f2), w_down,
      b_down.reshape(depth, n_exp, 1, d))


def _combine_ln_kernel(dest_ref, y_ref, gate_ref, res_ref, g_ref, b_ref, o_ref, buf, sems,
                       *, alpha):
    tt = res_ref.shape[0]
    i = pl.program_id(0)
    n = pl.num_programs(0)
    slot = i % 2

    def issue(step, s):
        def body(g, carry):
            row0 = pl.multiple_of(g * SUBLANES, SUBLANES)
            slot0 = (step * tt + row0) * TOP_K
            for j in range(SUBLANES):
                for k in range(TOP_K):
                    pltpu.make_async_copy(
                        y_ref.at[pl.ds(dest_ref[slot0 + (j * TOP_K + k)], 1), :],
                        buf.at[s, k, pl.ds(row0 + j, 1), :],
                        sems.at[s, k]).start(priority=k % 2)
            return carry
        lax.fori_loop(0, tt // SUBLANES, body, 0)

    @pl.when(i == 0)
    def _():
        issue(0, 0)

    for s in range(2):
        @pl.when((i + 1 < n) & (slot != s))
        def _(s=s):
            issue(i + 1, s)

    for k in range(TOP_K):
        pltpu.make_async_copy(y_ref.at[pl.ds(0, tt), :], buf.at[slot, k], sems.at[slot, k]).wait()

    gate = gate_ref[...]
    ffn = buf[slot, 0] * gate[:, 0:1]
    for k in range(1, TOP_K):
        ffn = ffn + buf[slot, k] * gate[:, k:k + 1]
    o_ref[...] = _layer_norm(alpha * res_ref[...] + ffn, g_ref[...], b_ref[...])


def _combine_ln(y_rows, dest_flat, gates, res, g, b, *, alpha, tt=128):
    t, d = res.shape
    assert t % tt == 0
    grid_spec = pltpu.PrefetchScalarGridSpec(
        num_scalar_prefetch=1,
        grid=(t // tt,),
        in_specs=[pl.BlockSpec(memory_space=pl.ANY),
                  pl.BlockSpec((tt, TOP_K), lambda i, dest: (i, 0)),
                  pl.BlockSpec((tt, d), lambda i, dest: (i, 0)),
                  pl.BlockSpec((1, d), lambda i, dest: (0, 0)),
                  pl.BlockSpec((1, d), lambda i, dest: (0, 0))],
        out_specs=pl.BlockSpec((tt, d), lambda i, dest: (i, 0)),
        scratch_shapes=[pltpu.VMEM((2, TOP_K, tt, d), F32),
                        pltpu.SemaphoreType.DMA((2, TOP_K))],
    )
    return pl.pallas_call(
        functools.partial(_combine_ln_kernel, alpha=alpha),
        grid_spec=grid_spec,
        out_shape=jax.ShapeDtypeStruct((t, d), F32),
        compiler_params=_params(("arbitrary",)),
        name="moe_combine_ln",
    )(dest_flat, y_rows, gates, res, g.reshape(1, d), b.reshape(1, d))


def _route(top_e, rank, counts, tm, n_blocks):
    n_exp = counts.shape[-1]
    counts = counts.reshape(n_exp).astype(jnp.int32)
    padded = (counts + tm - 1) // tm * tm
    pad_end = jnp.cumsum(padded)
    pad_start = pad_end - padded
    hit = top_e[:, :, None] == jnp.arange(n_exp, dtype=jnp.int32)
    dest = rank + jnp.sum(jnp.where(hit, pad_start, 0), axis=-1, dtype=jnp.int32)
    first_row = jnp.arange(n_blocks, dtype=jnp.int32) * tm
    block_e = jnp.minimum(jnp.sum(pad_end[None, :] <= first_row[:, None], axis=1, dtype=jnp.int32),
                          n_exp - 1)
    n_valid = (pad_end[-1] // tm).astype(jnp.int32).reshape(1)
    ids = jnp.arange(n_exp, dtype=jnp.int32)
    later = (ids[None, :] > ids[:, None]) & (counts[None, :] > 0)
    next_of = jnp.min(jnp.where(later, ids[None, :], n_exp), axis=1)
    next_of = jnp.where(next_of < n_exp, next_of, -1)
    next_e = jnp.sum(jnp.where(block_e[:, None] == ids[None, :], next_of[None, :], 0), axis=1,
                     dtype=jnp.int32)
    empty_lo = jnp.concatenate([pad_start + counts, pad_end[-1:]])
    empty_hi = jnp.concatenate([pad_end, jnp.full((1,), n_blocks * tm, jnp.int32)])
    return dest.reshape(-1), empty_lo, empty_hi, block_e, n_valid, next_e


def _moe_ln(x, top_e, gates, rank, counts, w_gu, b_gu, w_down, b_down, g, b, *, layer, alpha,
            tm=256):
    t, _ = x.shape
    n_exp = w_gu.shape[1]
    n_blocks = t * TOP_K // tm + n_exp
    dest, pad_lo, pad_hi, block_e, n_valid, next_e = _route(top_e, rank, counts, tm, n_blocks)
    xs = _dispatch(x, dest, pad_lo, pad_hi, n_blocks * tm)
    y_rows = _experts(xs, block_e, n_valid, next_e, w_gu, b_gu, w_down, b_down, layer=layer,
                      tm=tm)
    return _combine_ln(y_rows, dest, gates, x, g, b, alpha=alpha)


def _attn_kernel(*refs, scale):
    ng = len(DIL_GROUPS)
    q_refs = refs[0:ng]
    kc_refs = refs[ng:2 * ng]
    kp_refs = refs[2 * ng:3 * ng]
    vc_refs = refs[3 * ng:4 * ng]
    vp_refs = refs[4 * ng:5 * ng]
    o_ref = refs[5 * ng]
    og_s, lse_s = refs[5 * ng + 1:]
    blk = ATTN_BLK
    nblk = ATTN_SUPER // blk
    qi = lax.broadcasted_iota(jnp.int32, (nblk, blk, blk), 1)
    kj = lax.broadcasted_iota(jnp.int32, (nblk, blk, blk), 2)
    cur_ok = kj <= qi
    prev_ok = kj >= qi
    bidx = lax.broadcasted_iota(jnp.int32, (nblk, blk, blk), 0)
    seq_start = jnp.full((nblk, blk, blk), pl.program_id(1), jnp.int32) == 0
    qk = (((2,), (2,)), ((0,), (0,)))
    pv = (((2,), (1,)), ((0,), (0,)))

    for g, (_, dil) in enumerate(DIL_GROUPS):
        n_sub = nblk // dil
        span = n_sub * blk

        def blocks(ref, r, start_blk=0, count=n_sub, dil=dil):
            if dil == 1:
                rows = ref[start_blk * blk:(start_blk + count) * blk, :]
            else:
                rows = ref[pl.ds(r + dil * blk * start_blk, count * blk, stride=dil), :]
            return rows.reshape(count, blk, HEAD_DIM).astype(BF16)

        def with_prev(cur_ref, prev_ref, r, n_sub=n_sub):
            cur = blocks(cur_ref, r)
            last_of_prev = blocks(prev_ref, r, n_sub - 1, 1)
            if n_sub == 1:
                return cur, last_of_prev
            return cur, jnp.concatenate([last_of_prev, cur[:n_sub - 1]], axis=0)

        q, k_cur, k_prev, v_cur, v_prev = [], [], [], [], []
        for r in range(dil):
            q.append(blocks(q_refs[g], r))
            kc, kp = with_prev(kc_refs[g], kp_refs[g], r)
            vc, vp = with_prev(vc_refs[g], vp_refs[g], r)
            k_cur.append(kc)
            k_prev.append(kp)
            v_cur.append(vc)
            v_prev.append(vp)
        cat = lambda parts: parts[0] if len(parts) == 1 else jnp.concatenate(parts, axis=0)
        q, k_cur, k_prev, v_cur, v_prev = map(cat, (q, k_cur, k_prev, v_cur, v_prev))

        no_prev = seq_start & (bidx % n_sub == 0)
        s_cur = lax.dot_general(q, k_cur, qk, preferred_element_type=F32) * scale
        s_prev = lax.dot_general(q, k_prev, qk, preferred_element_type=F32) * scale
        s_cur = jnp.where(cur_ok, s_cur, -jnp.inf)
        s_prev = jnp.where(prev_ok & jnp.logical_not(no_prev), s_prev, -jnp.inf)
        m = jnp.maximum(jnp.max(s_cur, axis=-1, keepdims=True),
                        jnp.max(s_prev, axis=-1, keepdims=True))
        p_cur = jnp.exp(s_cur - m)
        p_prev = jnp.exp(s_prev - m)
        den = (jnp.sum(p_cur, axis=-1, keepdims=True)
               + jnp.sum(p_prev, axis=-1, keepdims=True))
        o = (lax.dot_general(p_cur.astype(BF16), v_cur, pv, preferred_element_type=F32)
             + lax.dot_general(p_prev.astype(BF16), v_prev, pv, preferred_element_type=F32)) / den
        lse = jnp.broadcast_to(m + jnp.log(den), (nblk, blk, LANES))
        for r in range(dil):
            o_r = o[r * n_sub:(r + 1) * n_sub].reshape(span, HEAD_DIM)
            lse_r = lse[r * n_sub:(r + 1) * n_sub].reshape(span, LANES)
            if dil == 1:
                og_s[g] = o_r
                lse_s[g] = lse_r
            else:
                og_s[g, pl.ds(r, span, stride=dil), :] = o_r
                lse_s[g, pl.ds(r, span, stride=dil), :] = lse_r

    top = lse_s[0]
    for g in range(1, ng):
        top = jnp.maximum(top, lse_s[g])
    num = jnp.zeros(top.shape, F32)
    den = jnp.zeros(top.shape, F32)
    for g in range(ng):
        w = jnp.exp(lse_s[g] - top)
        num = num + w * og_s[g]
        den = den + w
    o_ref[...] = (num / den).astype(o_ref.dtype)


def _attention(q, k, v, *, batch, hpg):
    t = q.shape[0]
    seq = t // batch
    assert seq % ATTN_SUPER == 0
    ns = seq // ATTN_SUPER
    ng = len(DIL_GROUPS)

    def cur(off):
        return lambda b, m, h, off=off: (b * ns + m, off + h)

    def prev(off):
        return lambda b, m, h, off=off: (b * ns + jnp.maximum(m - 1, 0), off + h)

    blk = lambda index_map: pl.BlockSpec((ATTN_SUPER, HEAD_DIM), index_map)
    in_specs = ([blk(cur(g * hpg)) for g in range(ng)]
                + [blk(cur(g * hpg)) for g in range(ng)]
                + [blk(prev(g * hpg)) for g in range(ng)]
                + [blk(cur(g * hpg)) for g in range(ng)]
                + [blk(prev(g * hpg)) for g in range(ng)])
    return pl.pallas_call(
        functools.partial(_attn_kernel, scale=HEAD_DIM ** -0.5),
        grid=(batch, ns, hpg),
        in_specs=in_specs,
        out_specs=pl.BlockSpec((ATTN_SUPER, HEAD_DIM), lambda b, m, h: (b * ns + m, h)),
        out_shape=jax.ShapeDtypeStruct((t, hpg * HEAD_DIM), BF16),
        scratch_shapes=[pltpu.VMEM((ng, ATTN_SUPER, HEAD_DIM), F32),
                        pltpu.VMEM((ng, ATTN_SUPER, LANES), F32)],
        compiler_params=_params(("parallel", "arbitrary", "arbitrary")),
        name="dilated_attn",
    )(*([q] * ng + [k] * (2 * ng) + [v] * (2 * ng)))


def _rope_tables(positions):
    half = ROT_DIM // 2
    inv = jnp.power(ROPE_THETA, -jnp.arange(half, dtype=F32) * (2.0 / ROT_DIM))
    ang = positions.reshape(-1).astype(F32)[:, None] * inv
    cos, sin = jnp.cos(ang), jnp.sin(ang)
    rest = HEAD_DIM - ROT_DIM
    cos_t = jnp.concatenate([cos, cos, jnp.ones((ang.shape[0], rest), F32)], axis=1)
    sin_t = jnp.concatenate([-sin, sin, jnp.zeros((ang.shape[0], rest), F32)], axis=1)
    return cos_t, sin_t


def kernel(x, positions, a_w_in, a_conv_w, a_conv_b, a_w_rg, a_b_rg, a_w_ig, a_b_ig, a_lambda,
           a_w_out, kv_w, b_w_q, b_w_o, mix_ln_g, mix_ln_b, moe_w_router, moe_b_router,
           moe_w_gu, moe_b_gu, moe_w_down, moe_b_down, ffn_ln_g, ffn_ln_b):
    batch, seq, d = x.shape
    depth = mix_ln_g.shape[0]
    n_a = a_w_in.shape[0]
    c = a_w_in.shape[2] // 2
    attn_w = b_w_q.shape[2]
    hpg = b_w_o.shape[1] // HEAD_DIM
    alpha = (2 * depth) ** 0.25
    t = batch * seq

    x = x.reshape(t, d)
    cos_t, sin_t = _rope_tables(positions)
    k_sh = v_sh = None
    for layer in range(depth):
        if layer < n_a:
            gate = _proj(x, a_w_in[layer][:, :c].astype(BF16), epilogue="gelu")
            u = _proj(x, a_w_in[layer][:, c:].astype(BF16))
            mixed = _rglru(gate, u, a_conv_w[layer], a_conv_b[layer], a_w_rg[layer], a_b_rg[layer],
                           a_w_ig[layer], a_b_ig[layer], a_lambda[layer], batch=batch)
            w_out = a_w_out[layer]
        else:
            j = layer - n_a
            q = _proj(x, b_w_q[j].astype(BF16), epilogue="rope", cos_t=cos_t, sin_t=sin_t)
            mixed = _attention(q, k_sh, v_sh, batch=batch, hpg=hpg)
            w_out = b_w_o[j]
        x, top_e, gates, rank, counts = _mix_ln_router(
            mixed, w_out.astype(BF16), x, mix_ln_g[layer], mix_ln_b[layer], moe_w_router[layer],
            moe_b_router[layer], alpha=alpha)
        x = _moe_ln(x, top_e, gates, rank, counts, moe_w_gu, moe_b_gu, moe_w_down, moe_b_down,
                    ffn_ln_g[layer], ffn_ln_b[layer], layer=layer, alpha=alpha)
        if layer == n_a - 1:
            k_sh = _proj(x, kv_w[:, :attn_w].astype(BF16), epilogue="rope",
                         cos_t=cos_t, sin_t=sin_t)
            v_sh = _proj(x, kv_w[:, attn_w:].astype(BF16))
    return x.reshape(batch, seq, d)
```

```python
import functools

import jax
import jax.numpy as jnp
from jax import lax
from jax.experimental import pallas as pl
from jax.experimental.pallas import tpu as pltpu

F32 = jnp.float32
BF16 = jnp.bfloat16

CONV_W = 4
LRU_C = 8.0
HEAD_DIM = 128
ROT_DIM = HEAD_DIM // 4
ROPE_THETA = 500000.0
DIL_GROUPS = ((128, 1), (512, 4), (2048, 16))
ATTN_BLK = 128
ATTN_SUPER = 2048
TOP_K = 4
SWIGLU_LIMIT = 7.0
SWIGLU_ALPHA = 1.702
LN_EPS = 1e-5

LANES = 128
SUBLANES = 8
VMEM_LIMIT = 56 * 1024 * 1024


def _params(sem):
    return pltpu.CompilerParams(dimension_semantics=sem, vmem_limit_bytes=VMEM_LIMIT)


def _pick(n, candidates):
    for c in candidates:
        if n % c == 0:
            return c
    raise ValueError(f"no tile in {candidates} divides {n}")


def _layer_norm(y, g, b):
    mu = jnp.mean(y, axis=-1, keepdims=True)
    yc = y - mu
    var = jnp.mean(yc * yc, axis=-1, keepdims=True)
    return yc * lax.rsqrt(var + LN_EPS) * g + b


def _rope_chunk(c, cos_t, sin_t):
    lane = lax.broadcasted_iota(jnp.int32, c.shape, 1)
    partner = jnp.where(lane < ROT_DIM // 2,
                        pltpu.roll(c, LANES - ROT_DIM // 2, 1),
                        pltpu.roll(c, ROT_DIM // 2, 1))
    return c * cos_t + partner * sin_t


def _proj_kernel(x_ref, w_ref, *rest, epilogue):
    if epilogue == "rope":
        cos_ref, sin_ref, o_ref = rest
    else:
        (o_ref,) = rest
    acc = jnp.dot(x_ref[...].astype(BF16), w_ref[...], preferred_element_type=F32)
    if epilogue == "gelu":
        o_ref[...] = 0.5 * acc * (1.0 + jnp.tanh(
            0.7978845608028654 * (acc + 0.044715 * (acc * acc * acc))))
    elif epilogue == "rope":
        cos_t = cos_ref[...]
        sin_t = sin_ref[...]
        for h in range(acc.shape[1] // LANES):
            sl = slice(h * LANES, (h + 1) * LANES)
            o_ref[:, sl] = _rope_chunk(acc[:, sl], cos_t, sin_t)
    else:
        o_ref[...] = acc


def _proj(x, w, *, epilogue="none", cos_t=None, sin_t=None, tm=256):
    m, k = x.shape
    n = w.shape[1]
    assert m % tm == 0 and n % LANES == 0
    in_specs = [pl.BlockSpec((tm, k), lambda i: (i, 0)),
                pl.BlockSpec((k, n), lambda i: (0, 0))]
    args = [x, w]
    if epilogue == "rope":
        in_specs += [pl.BlockSpec((tm, LANES), lambda i: (i, 0))] * 2
        args += [cos_t, sin_t]
    return pl.pallas_call(
        functools.partial(_proj_kernel, epilogue=epilogue),
        grid=(m // tm,),
        in_specs=in_specs,
        out_specs=pl.BlockSpec((tm, n), lambda i: (i, 0)),
        out_shape=jax.ShapeDtypeStruct((m, n), F32),
        compiler_params=_params(("parallel",)),
        name=f"proj_{epilogue}",
    )(*args)


def _band_plan(c, bw, ct):
    plan = []
    for j in range(c // ct):
        c0 = j * ct
        lo = (c0 // bw) * bw
        hi = ((c0 + ct - 1) // bw + 1) * bw
        ks = (lo // LANES) * LANES
        kb = -(-(hi - ks) // LANES) * LANES
        ks = min(ks, c - kb)
        plan.append((ks, kb))
    return plan


def _band_weights(w_blocks, plan, ct):
    nb, bw, _ = w_blocks.shape
    c = nb * bw
    dense = jnp.zeros((nb, bw, nb, bw), w_blocks.dtype)
    dense = dense.at[jnp.arange(nb), :, jnp.arange(nb), :].set(w_blocks)
    dense = dense.reshape(c, c)
    kb_max = max(kb for _, kb in plan)
    slabs = []
    for j, (ks, kb) in enumerate(plan):
        slab = dense[ks:ks + kb, j * ct:(j + 1) * ct]
        slabs.append(jnp.pad(slab, ((0, kb_max - kb), (0, 0))))
    return jnp.stack(slabs).astype(BF16)


def _sigmoid(x):
    return 0.5 * jnp.tanh(0.5 * x) + 0.5


def _scan8(a, b):
    row = lax.broadcasted_iota(jnp.int32, a.shape, 0)
    for d in (1, 2, 4):
        a_prev = pltpu.roll(a, d, 0)
        b_prev = pltpu.roll(b, d, 0)
        keep = row >= d
        b = jnp.where(keep, a * b_prev + b, b)
        a = jnp.where(keep, a * a_prev, a)
    return a, b


def _rglru_kernel(gate_ref, u_ref, cw_ref, cb_ref, wrg_ref, brg_ref, wig_ref, big_ref,
                  lam_ref, o_ref, ubuf, uc_s, hcarry, *, plan, ct):
    ts = u_ref.shape[0]
    i = pl.program_id(1)

    @pl.when(i == 0)
    def _():
        ubuf[0:SUBLANES, :] = jnp.zeros((SUBLANES, ubuf.shape[1]), F32)
        hcarry[...] = jnp.zeros(hcarry.shape, F32)

    ubuf[SUBLANES:SUBLANES + ts, :] = u_ref[...]
    base = SUBLANES - (CONV_W - 1)
    conv = ubuf[base:base + ts, :] * cw_ref[0:1, :]
    for k in range(1, CONV_W):
        conv = conv + ubuf[base + k:base + k + ts, :] * cw_ref[k:k + 1, :]
    uc_s[...] = cb_ref[...] + conv
    ubuf[0:SUBLANES, :] = ubuf[ts:ts + SUBLANES, :]

    lam = lam_ref[...]
    softplus_neg_lam = jnp.maximum(-lam, 0.0) + jnp.log1p(jnp.exp(-jnp.abs(lam)))

    for j, (ks, kb) in enumerate(plan):
        cs = slice(j * ct, (j + 1) * ct)
        band = uc_s[:, ks:ks + kb].astype(BF16)
        r_pre = jnp.dot(band, wrg_ref[j, 0:kb, :], preferred_element_type=F32) + brg_ref[:, cs]
        i_pre = jnp.dot(band, wig_ref[j, 0:kb, :], preferred_element_type=F32) + big_ref[:, cs]
        carry = hcarry[:, cs]
        for c0 in range(0, ts, SCAN_ROWS):
            rs = slice(c0, c0 + SCAN_ROWS)
            log_a = -LRU_C * _sigmoid(r_pre[rs]) * softplus_neg_lam[:, cs]
            a = jnp.exp(log_a)
            b = (jnp.sqrt(-jnp.tanh(log_a) * (a * a + 1.0)) * _sigmoid(i_pre[rs])
                 * uc_s[rs, cs])
            outs = []
            for s0 in range(0, SCAN_ROWS, SUBLANES):
                a8, b8 = _scan8(a[s0:s0 + SUBLANES], b[s0:s0 + SUBLANES])
                h8 = b8 + a8 * carry
                carry = jnp.broadcast_to(h8[SUBLANES - 1:SUBLANES, :], h8.shape)
                outs.append(h8 * gate_ref[c0 + s0:c0 + s0 + SUBLANES, cs])
            o_ref[rs, cs] = jnp.concatenate(outs, axis=0).astype(o_ref.dtype)
        hcarry[:, cs] = carry


SCAN_ROWS = 64


def _rglru(gate, u, conv_w, conv_b, w_rg, b_rg, w_ig, b_ig, lam, *, batch, ts=256):
    t, c = u.shape
    nb, bw, _ = w_rg.shape
    ct = _pick(c, (384, 256, 128))
    plan = _band_plan(c, bw, ct)
    wrg = _band_weights(w_rg, plan, ct)
    wig = _band_weights(w_ig, plan, ct)
    seq = t // batch
    assert seq % ts == 0
    nt = seq // ts
    row = lambda v: v.reshape(1, c)
    vec = pl.BlockSpec((1, c), lambda b, i: (0, 0))
    slab = pl.BlockSpec(wrg.shape, lambda b, i: (0, 0, 0))
    return pl.pallas_call(
        functools.partial(_rglru_kernel, plan=plan, ct=ct),
        grid=(batch, nt),
        in_specs=[pl.BlockSpec((ts, c), lambda b, i: (b * nt + i, 0)),
                  pl.BlockSpec((ts, c), lambda b, i: (b * nt + i, 0)),
                  pl.BlockSpec((CONV_W, c), lambda b, i: (0, 0)),
                  vec, slab, vec, slab, vec, vec],
        out_specs=pl.BlockSpec((ts, c), lambda b, i: (b * nt + i, 0)),
        out_shape=jax.ShapeDtypeStruct((t, c), BF16),
        scratch_shapes=[pltpu.VMEM((ts + SUBLANES, c), F32),
                        pltpu.VMEM((ts, c), F32),
                        pltpu.VMEM((SUBLANES, c), F32)],
        compiler_params=_params(("arbitrary", "arbitrary")),
        name="rglru",
    )(gate, u, conv_w, row(conv_b), wrg, row(b_rg), wig, row(b_ig), row(lam))


def _mix_ln_router_kernel(a_ref, w_ref, res_ref, g_ref, b_ref, wr2_ref, br_ref,
                          xn_ref, e_ref, gate_ref, rank_ref, count_ref, run_s, *, alpha):
    tm = a_ref.shape[0]
    n_exp = br_ref.shape[1]

    @pl.when(pl.program_id(0) == 0)
    def _():
        run_s[...] = jnp.zeros(run_s.shape, F32)

    mix = jnp.dot(a_ref[...], w_ref[...], preferred_element_type=F32)
    xn = _layer_norm(alpha * res_ref[...] + mix, g_ref[...], b_ref[...])
    xn_ref[...] = xn

    x_hi = xn.astype(BF16)
    x_lo = (xn - x_hi.astype(F32)).astype(BF16)
    prod = jnp.dot(jnp.concatenate([x_hi, x_lo], axis=0), wr2_ref[...],
                   preferred_element_type=F32)
    logits = ((prod[:tm, n_exp:] + prod[tm:, :n_exp] + prod[tm:, n_exp:]) + prod[:tm, :n_exp]
              + br_ref[...])

    col = lax.broadcasted_iota(jnp.int32, logits.shape, 1).astype(F32)
    slot = lax.broadcasted_iota(jnp.int32, e_ref.shape, 1)
    e_out = jnp.zeros(e_ref.shape, F32)
    v_out = jnp.zeros(e_ref.shape, F32)
    picks = []
    for k in range(TOP_K):
        top = jnp.max(logits, axis=-1, keepdims=True)
        idx = jnp.min(jnp.where(logits == top, col, float(n_exp)), axis=-1, keepdims=True)
        picks.append(col == idx)
        logits = jnp.where(picks[k], -jnp.inf, logits)
        e_out = jnp.where(slot == k, idx, e_out)
        v_out = jnp.where(slot == k, top, v_out)
    p = jnp.exp(v_out - jnp.max(v_out, axis=-1, keepdims=True))
    e_ref[...] = e_out.astype(jnp.int32)
    gate_ref[...] = p / jnp.sum(p, axis=-1, keepdims=True)

    sel = jnp.zeros(logits.shape, F32)
    for k in range(TOP_K):
        sel = jnp.where(picks[k], 1.0, sel)
    earlier = (lax.broadcasted_iota(jnp.int32, (tm, tm), 1)
               < lax.broadcasted_iota(jnp.int32, (tm, tm), 0))
    before = jnp.dot(jnp.where(earlier, 1.0, 0.0).astype(BF16), sel.astype(BF16),
                     preferred_element_type=F32) + run_s[...]
    rank = jnp.zeros(e_ref.shape, F32)
    for k in range(TOP_K):
        rank_k = jnp.sum(jnp.where(picks[k], before, 0.0), axis=-1, keepdims=True)
        rank = jnp.where(slot == k, rank_k, rank)
    rank_ref[...] = rank.astype(jnp.int32)
    run_s[...] = run_s[...] + jnp.sum(sel, axis=0, keepdims=True)
    count_ref[...] = run_s[...]


def _mix_ln_router(a, w, res, g, b, w_router, b_router, *, alpha, tm=256):
    m, k = a.shape
    d = w.shape[1]
    n_exp = w_router.shape[1]
    assert m % tm == 0
    w_hi = w_router.astype(BF16)
    w_lo = (w_router - w_hi.astype(F32)).astype(BF16)
    full = lambda shape: pl.BlockSpec(shape, lambda i: (0,) * len(shape))
    return pl.pallas_call(
        functools.partial(_mix_ln_router_kernel, alpha=alpha),
        grid=(m // tm,),
        in_specs=[pl.BlockSpec((tm, k), lambda i: (i, 0)), full((k, d)),
                  pl.BlockSpec((tm, d), lambda i: (i, 0)), full((1, d)), full((1, d)),
                  full((d, 2 * n_exp)), full((1, n_exp))],
        out_specs=[pl.BlockSpec((tm, d), lambda i: (i, 0)),
                   pl.BlockSpec((tm, TOP_K), lambda i: (i, 0)),
                   pl.BlockSpec((tm, TOP_K), lambda i: (i, 0)),
                   pl.BlockSpec((tm, TOP_K), lambda i: (i, 0)),
                   full((1, n_exp))],
        out_shape=[jax.ShapeDtypeStruct((m, d), F32),
                   jax.ShapeDtypeStruct((m, TOP_K), jnp.int32),
                   jax.ShapeDtypeStruct((m, TOP_K), F32),
                   jax.ShapeDtypeStruct((m, TOP_K), jnp.int32),
                   jax.ShapeDtypeStruct((1, n_exp), F32)],
        scratch_shapes=[pltpu.VMEM((1, n_exp), F32)],
        compiler_params=_params(("arbitrary",)),
        name="mix_ln_router",
    )(a, w, res, g.reshape(1, d), b.reshape(1, d), jnp.concatenate([w_hi, w_lo], axis=1),
      b_router.reshape(1, n_exp))


def _dispatch_kernel(dest_ref, empty_lo_ref, empty_hi_ref, x_ref, xs_ref, zero_s, sems, zsem):
    tt = x_ref.shape[0]
    i = pl.program_id(0)

    def issue(g, carry):
        row0 = pl.multiple_of(g * SUBLANES, SUBLANES)
        slot0 = (i * tt + row0) * TOP_K
        for j in range(SUBLANES):
            for k in range(TOP_K):
                pltpu.make_async_copy(
                    x_ref.at[pl.ds(row0 + j, 1), :],
                    xs_ref.at[pl.ds(dest_ref[slot0 + (j * TOP_K + k)], 1), :],
                    sems.at[k]).start(priority=k % 2)
        return carry

    lax.fori_loop(0, tt // SUBLANES, issue, 0)

    for k in range(TOP_K):
        pltpu.make_async_copy(x_ref, xs_ref.at[pl.ds(0, tt), :], sems.at[k]).wait()

    @pl.when(i == pl.num_programs(0) - 1)
    def _():
        zero_s[...] = jnp.zeros(zero_s.shape, zero_s.dtype)

        def zero_copy(row):
            return pltpu.make_async_copy(zero_s, xs_ref.at[pl.ds(row, 1), :], zsem)

        def fill(e, carry):
            def one(row, c):
                zero_copy(row).start()
                return c
            return lax.fori_loop(empty_lo_ref[e], empty_hi_ref[e], one, carry)

        def drain(e, carry):
            def one(row, c):
                zero_copy(row).wait()
                return c
            return lax.fori_loop(empty_lo_ref[e], empty_hi_ref[e], one, carry)

        lax.fori_loop(0, empty_lo_ref.shape[0], fill, 0)
        lax.fori_loop(0, empty_lo_ref.shape[0], drain, 0)


def _dispatch(x, dest_flat, empty_lo, empty_hi, n_rows, *, tt=512):
    t, d = x.shape
    assert t % tt == 0
    grid_spec = pltpu.PrefetchScalarGridSpec(
        num_scalar_prefetch=3,
        grid=(t // tt,),
        in_specs=[pl.BlockSpec((tt, d), lambda i, *_: (i, 0))],
        out_specs=pl.BlockSpec(memory_space=pl.ANY),
        scratch_shapes=[pltpu.VMEM((1, d), F32),
                        pltpu.SemaphoreType.DMA((TOP_K,)),
                        pltpu.SemaphoreType.DMA],
    )
    return pl.pallas_call(
        _dispatch_kernel,
        grid_spec=grid_spec,
        out_shape=jax.ShapeDtypeStruct((n_rows, d), F32),
        compiler_params=_params(("arbitrary",)),
        name="moe_dispatch",
    )(dest_flat, empty_lo, empty_hi, x)


def _expert_kernel(block_e_ref, n_valid_ref, next_e_ref, xs_ref, wgu_hbm, bgu_ref, wd_hbm, bd_ref,
                   y_ref, wgu_f, wd_f, wgu_b, wd_b, sems, *, layer):
    f = wd_b.shape[0]
    i = pl.program_id(0)
    live = i < n_valid_ref[0]
    expert = block_e_ref[i]

    def fetch(e):
        return (pltpu.make_async_copy(wgu_hbm.at[layer, e], wgu_f, sems.at[0]),
                pltpu.make_async_copy(wd_hbm.at[layer, e], wd_f, sems.at[1]))

    @pl.when(i == 0)
    def _():
        for copy in fetch(expert):
            copy.start()

    @pl.when(live & ((i == 0) | (expert != block_e_ref[jnp.maximum(i - 1, 0)])))
    def _():
        for copy in fetch(expert):
            copy.wait()
        wgu_b[...] = wgu_f[...].astype(BF16)
        wd_b[...] = wd_f[...].astype(BF16)

        @pl.when(next_e_ref[i] >= 0)
        def _():
            for copy in fetch(next_e_ref[i]):
                copy.start()

    @pl.when(jnp.logical_not(live))
    def _():
        y_ref[...] = jnp.zeros(y_ref.shape, y_ref.dtype)

    @pl.when(live)
    def _():
        gu = jnp.dot(xs_ref[...].astype(BF16), wgu_b[...],
                     preferred_element_type=F32) + bgu_ref[...]
        glu = jnp.minimum(gu[:, :f], SWIGLU_LIMIT)
        lin = jnp.clip(gu[:, f:], -SWIGLU_LIMIT, SWIGLU_LIMIT)
        h = glu * jax.nn.sigmoid(SWIGLU_ALPHA * glu) * (lin + 1.0)
        y_ref[...] = jnp.dot(h.astype(BF16), wd_b[...],
                             preferred_element_type=F32) + bd_ref[...]


def _experts(xs, block_e, n_valid, next_e, w_gu, b_gu, w_down, b_down, *, layer, tm):
    r, d = xs.shape
    depth, n_exp, _, f2 = w_gu.shape
    f = f2 // 2
    nb = r // tm
    live = lambda i, be, nv, ne: jnp.minimum(i, nv[0] - 1)
    grid_spec = pltpu.PrefetchScalarGridSpec(
        num_scalar_prefetch=3,
        grid=(nb,),
        in_specs=[pl.BlockSpec((tm, d), lambda i, be, nv, ne: (live(i, be, nv, ne), 0)),
                  pl.BlockSpec(memory_space=pl.ANY),
                  pl.BlockSpec((None, None, 1, f2), lambda i, be, nv, ne: (layer, be[i], 0, 0)),
                  pl.BlockSpec(memory_space=pl.ANY),
                  pl.BlockSpec((None, None, 1, d), lambda i, be, nv, ne: (layer, be[i], 0, 0))],
        out_specs=pl.BlockSpec((tm, d), lambda i, be, nv, ne: (i, 0)),
        scratch_shapes=[pltpu.VMEM((d, f2), F32), pltpu.VMEM((f, d), F32),
                        pltpu.VMEM((d, f2), BF16), pltpu.VMEM((f, d), BF16),
                        pltpu.SemaphoreType.DMA((2,))],
    )
    return pl.pallas_call(
        functools.partial(_expert_kernel, layer=layer),
        grid_spec=grid_spec,
        out_shape=jax.ShapeDtypeStruct((r, d), F32),
        compiler_params=_params(("arbitrary",)),
        name="moe_experts",
    )(block_e, n_valid, next_e, xs, w_gu, b_gu.reshape(depth, n_exp, 1, f2), w_down,
      b_down.reshape(depth, n_exp, 1, d))


def _combine_ln_kernel(dest_ref, y_ref, gate_ref, res_ref, g_ref, b_ref, o_ref, buf, sems,
                       *, alpha):
    tt = res_ref.shape[0]
    i = pl.program_id(0)
    n = pl.num_programs(0)
    slot = i % 2

    def issue(step, s):
        def body(g, carry):
            row0 = pl.multiple_of(g * SUBLANES, SUBLANES)
            slot0 = (step * tt + row0) * TOP_K
            for j in range(SUBLANES):
                for k in range(TOP_K):
                    pltpu.make_async_copy(
                        y_ref.at[pl.ds(dest_ref[slot0 + (j * TOP_K + k)], 1), :],
                        buf.at[s, k, pl.ds(row0 + j, 1), :],
                        sems.at[s, k]).start(priority=k % 2)
            return carry
        lax.fori_loop(0, tt // SUBLANES, body, 0)

    @pl.when(i == 0)
    def _():
        issue(0, 0)

    for s in range(2):
        @pl.when((i + 1 < n) & (slot != s))
        def _(s=s):
            issue(i + 1, s)

    for k in range(TOP_K):
        pltpu.make_async_copy(y_ref.at[pl.ds(0, tt), :], buf.at[slot, k], sems.at[slot, k]).wait()

    gate = gate_ref[...]
    ffn = buf[slot, 0] * gate[:, 0:1]
    for k in range(1, TOP_K):
        ffn = ffn + buf[slot, k] * gate[:, k:k + 1]
    o_ref[...] = _layer_norm(alpha * res_ref[...] + ffn, g_ref[...], b_ref[...])


def _combine_ln(y_rows, dest_flat, gates, res, g, b, *, alpha, tt=256):
    t, d = res.shape
    assert t % tt == 0
    grid_spec = pltpu.PrefetchScalarGridSpec(
        num_scalar_prefetch=1,
        grid=(t // tt,),
        in_specs=[pl.BlockSpec(memory_space=pl.ANY),
                  pl.BlockSpec((tt, TOP_K), lambda i, dest: (i, 0)),
                  pl.BlockSpec((tt, d), lambda i, dest: (i, 0)),
                  pl.BlockSpec((1, d), lambda i, dest: (0, 0)),
                  pl.BlockSpec((1, d), lambda i, dest: (0, 0))],
        out_specs=pl.BlockSpec((tt, d), lambda i, dest: (i, 0)),
        scratch_shapes=[pltpu.VMEM((2, TOP_K, tt, d), F32),
                        pltpu.SemaphoreType.DMA((2, TOP_K))],
    )
    return pl.pallas_call(
        functools.partial(_combine_ln_kernel, alpha=alpha),
        grid_spec=grid_spec,
        out_shape=jax.ShapeDtypeStruct((t, d), F32),
        compiler_params=_params(("arbitrary",)),
        name="moe_combine_ln",
    )(dest_flat, y_rows, gates, res, g.reshape(1, d), b.reshape(1, d))


def _route(top_e, rank, counts, tm, n_blocks):
    n_exp = counts.shape[-1]
    counts = counts.reshape(n_exp).astype(jnp.int32)
    padded = (counts + tm - 1) // tm * tm
    pad_end = jnp.cumsum(padded)
    pad_start = pad_end - padded
    hit = top_e[:, :, None] == jnp.arange(n_exp, dtype=jnp.int32)
    dest = rank + jnp.sum(jnp.where(hit, pad_start, 0), axis=-1, dtype=jnp.int32)
    first_row = jnp.arange(n_blocks, dtype=jnp.int32) * tm
    block_e = jnp.minimum(jnp.sum(pad_end[None, :] <= first_row[:, None], axis=1, dtype=jnp.int32),
                          n_exp - 1)
    n_valid = (pad_end[-1] // tm).astype(jnp.int32).reshape(1)
    ids = jnp.arange(n_exp, dtype=jnp.int32)
    later = (ids[None, :] > ids[:, None]) & (counts[None, :] > 0)
    next_of = jnp.min(jnp.where(later, ids[None, :], n_exp), axis=1)
    next_of = jnp.where(next_of < n_exp, next_of, -1)
    next_e = jnp.sum(jnp.where(block_e[:, None] == ids[None, :], next_of[None, :], 0), axis=1,
                     dtype=jnp.int32)
    empty_lo = jnp.concatenate([pad_start + counts, pad_end[-1:]])
    empty_hi = jnp.concatenate([pad_end, jnp.full((1,), n_blocks * tm, jnp.int32)])
    return dest.reshape(-1), empty_lo, empty_hi, block_e, n_valid, next_e


def _moe_ln(x, top_e, gates, rank, counts, w_gu, b_gu, w_down, b_down, g, b, *, layer, alpha,
            tm=256):
    t, _ = x.shape
    n_exp = w_gu.shape[1]
    n_blocks = t * TOP_K // tm + n_exp
    dest, pad_lo, pad_hi, block_e, n_valid, next_e = _route(top_e, rank, counts, tm, n_blocks)
    xs = _dispatch(x, dest, pad_lo, pad_hi, n_blocks * tm)
    y_rows = _experts(xs, block_e, n_valid, next_e, w_gu, b_gu, w_down, b_down, layer=layer,
                      tm=tm)
    return _combine_ln(y_rows, dest, gates, x, g, b, alpha=alpha)


def _attn_kernel(*refs, scale):
    ng = len(DIL_GROUPS)
    q_refs = refs[0:ng]
    kc_refs = refs[ng:2 * ng]
    kp_refs = refs[2 * ng:3 * ng]
    vc_refs = refs[3 * ng:4 * ng]
    vp_refs = refs[4 * ng:5 * ng]
    o_ref = refs[5 * ng]
    og_s, lse_s = refs[5 * ng + 1:]
    blk = ATTN_BLK
    nblk = ATTN_SUPER // blk
    qi = lax.broadcasted_iota(jnp.int32, (nblk, blk, blk), 1)
    kj = lax.broadcasted_iota(jnp.int32, (nblk, blk, blk), 2)
    cur_ok = kj <= qi
    prev_ok = kj >= qi
    bidx = lax.broadcasted_iota(jnp.int32, (nblk, blk, blk), 0)
    seq_start = jnp.full((nblk, blk, blk), pl.program_id(1), jnp.int32) == 0
    qk = (((2,), (2,)), ((0,), (0,)))
    pv = (((2,), (1,)), ((0,), (0,)))

    for g, (_, dil) in enumerate(DIL_GROUPS):
        n_sub = nblk // dil
        span = n_sub * blk

        def blocks(ref, r, start_blk=0, count=n_sub, dil=dil):
            if dil == 1:
                rows = ref[start_blk * blk:(start_blk + count) * blk, :]
            else:
                rows = ref[pl.ds(r + dil * blk * start_blk, count * blk, stride=dil), :]
            return rows.reshape(count, blk, HEAD_DIM).astype(BF16)

        def with_prev(cur_ref, prev_ref, r, n_sub=n_sub):
            cur = blocks(cur_ref, r)
            last_of_prev = blocks(prev_ref, r, n_sub - 1, 1)
            if n_sub == 1:
                return cur, last_of_prev
            return cur, jnp.concatenate([last_of_prev, cur[:n_sub - 1]], axis=0)

        q, k_cur, k_prev, v_cur, v_prev = [], [], [], [], []
        for r in range(dil):
            q.append(blocks(q_refs[g], r))
            kc, kp = with_prev(kc_refs[g], kp_refs[g], r)
            vc, vp = with_prev(vc_refs[g], vp_refs[g], r)
            k_cur.append(kc)
            k_prev.append(kp)
            v_cur.append(vc)
            v_prev.append(vp)
        cat = lambda parts: parts[0] if len(parts) == 1 else jnp.concatenate(parts, axis=0)
        q, k_cur, k_prev, v_cur, v_prev = map(cat, (q, k_cur, k_prev, v_cur, v_prev))

        no_prev = seq_start & (bidx % n_sub == 0)
        s_cur = lax.dot_general(q, k_cur, qk, preferred_element_type=F32) * scale
        s_prev = lax.dot_general(q, k_prev, qk, preferred_element_type=F32) * scale
        s_cur = jnp.where(cur_ok, s_cur, -jnp.inf)
        s_prev = jnp.where(prev_ok & jnp.logical_not(no_prev), s_prev, -jnp.inf)
        m = jnp.maximum(jnp.max(s_cur, axis=-1, keepdims=True),
                        jnp.max(s_prev, axis=-1, keepdims=True))
        p_cur = jnp.exp(s_cur - m)
        p_prev = jnp.exp(s_prev - m)
        den = (jnp.sum(p_cur, axis=-1, keepdims=True)
               + jnp.sum(p_prev, axis=-1, keepdims=True))
        o = (lax.dot_general(p_cur.astype(BF16), v_cur, pv, preferred_element_type=F32)
             + lax.dot_general(p_prev.astype(BF16), v_prev, pv, preferred_element_type=F32)) / den
        lse = jnp.broadcast_to(m + jnp.log(den), (nblk, blk, LANES))
        for r in range(dil):
            o_r = o[r * n_sub:(r + 1) * n_sub].reshape(span, HEAD_DIM)
            lse_r = lse[r * n_sub:(r + 1) * n_sub].reshape(span, LANES)
            if dil == 1:
                og_s[g] = o_r
                lse_s[g] = lse_r
            else:
                og_s[g, pl.ds(r, span, stride=dil), :] = o_r
                lse_s[g, pl.ds(r, span, stride=dil), :] = lse_r

    top = lse_s[0]
    for g in range(1, ng):
        top = jnp.maximum(top, lse_s[g])
    num = jnp.zeros(top.shape, F32)
    den = jnp.zeros(top.shape, F32)
    for g in range(ng):
        w = jnp.exp(lse_s[g] - top)
        num = num + w * og_s[g]
        den = den + w
    o_ref[...] = (num / den).astype(o_ref.dtype)


def _attention(q, k, v, *, batch, hpg):
    t = q.shape[0]
    seq = t // batch
    assert seq % ATTN_SUPER == 0
    ns = seq // ATTN_SUPER
    ng = len(DIL_GROUPS)

    def cur(off):
        return lambda b, m, h, off=off: (b * ns + m, off + h)

    def prev(off):
        return lambda b, m, h, off=off: (b * ns + jnp.maximum(m - 1, 0), off + h)

    blk = lambda index_map: pl.BlockSpec((ATTN_SUPER, HEAD_DIM), index_map)
    in_specs = ([blk(cur(g * hpg)) for g in range(ng)]
                + [blk(cur(g * hpg)) for g in range(ng)]
                + [blk(prev(g * hpg)) for g in range(ng)]
                + [blk(cur(g * hpg)) for g in range(ng)]
                + [blk(prev(g * hpg)) for g in range(ng)])
    return pl.pallas_call(
        functools.partial(_attn_kernel, scale=HEAD_DIM ** -0.5),
        grid=(batch, ns, hpg),
        in_specs=in_specs,
        out_specs=pl.BlockSpec((ATTN_SUPER, HEAD_DIM), lambda b, m, h: (b * ns + m, h)),
        out_shape=jax.ShapeDtypeStruct((t, hpg * HEAD_DIM), BF16),
        scratch_shapes=[pltpu.VMEM((ng, ATTN_SUPER, HEAD_DIM), F32),
                        pltpu.VMEM((ng, ATTN_SUPER, LANES), F32)],
        compiler_params=_params(("parallel", "arbitrary", "arbitrary")),
        name="dilated_attn",
    )(*([q] * ng + [k] * (2 * ng) + [v] * (2 * ng)))


def _rope_tables(positions):
    half = ROT_DIM // 2
    inv = jnp.power(ROPE_THETA, -jnp.arange(half, dtype=F32) * (2.0 / ROT_DIM))
    ang = positions.reshape(-1).astype(F32)[:, None] * inv
    cos, sin = jnp.cos(ang), jnp.sin(ang)
    rest = HEAD_DIM - ROT_DIM
    cos_t = jnp.concatenate([cos, cos, jnp.ones((ang.shape[0], rest), F32)], axis=1)
    sin_t = jnp.concatenate([-sin, sin, jnp.zeros((ang.shape[0], rest), F32)], axis=1)
    return cos_t, sin_t


def kernel(x, positions, a_w_in, a_conv_w, a_conv_b, a_w_rg, a_b_rg, a_w_ig, a_b_ig, a_lambda,
           a_w_out, kv_w, b_w_q, b_w_o, mix_ln_g, mix_ln_b, moe_w_router, moe_b_router,
           moe_w_gu, moe_b_gu, moe_w_down, moe_b_down, ffn_ln_g, ffn_ln_b):
    batch, seq, d = x.shape
    depth = mix_ln_g.shape[0]
    n_a = a_w_in.shape[0]
    c = a_w_in.shape[2] // 2
    attn_w = b_w_q.shape[2]
    hpg = b_w_o.shape[1] // HEAD_DIM
    alpha = (2 * depth) ** 0.25
    t = batch * seq

    x = x.reshape(t, d)
    cos_t, sin_t = _rope_tables(positions)
    k_sh = v_sh = None
    for layer in range(depth):
        if layer < n_a:
            gate = _proj(x, a_w_in[layer][:, :c].astype(BF16), epilogue="gelu")
            u = _proj(x, a_w_in[layer][:, c:].astype(BF16))
            mixed = _rglru(gate, u, a_conv_w[layer], a_conv_b[layer], a_w_rg[layer], a_b_rg[layer],
                           a_w_ig[layer], a_b_ig[layer], a_lambda[layer], batch=batch)
            w_out = a_w_out[layer]
        else:
            j = layer - n_a
            q = _proj(x, b_w_q[j].astype(BF16), epilogue="rope", cos_t=cos_t, sin_t=sin_t)
            mixed = _attention(q, k_sh, v_sh, batch=batch, hpg=hpg)
            w_out = b_w_o[j]
        x, top_e, gates, rank, counts = _mix_ln_router(
            mixed, w_out.astype(BF16), x, mix_ln_g[layer], mix_ln_b[layer], moe_w_router[layer],
            moe_b_router[layer], alpha=alpha)
        x = _moe_ln(x, top_e, gates, rank, counts, moe_w_gu, moe_b_gu, moe_w_down, moe_b_down,
                    ffn_ln_g[layer], ffn_ln_b[layer], layer=layer, alpha=alpha)
        if layer == n_a - 1:
            k_sh = _proj(x, kv_w[:, :attn_w].astype(BF16), epilogue="rope",
                         cos_t=cos_t, sin_t=sin_t)
            v_sh = _proj(x, kv_w[:, attn_w:].astype(BF16))
    return x.reshape(batch, seq, d)
```
